```python
import math
import jax
import jax.numpy as jnp
from jax import lax
import numpy as np

D_MODEL = 1024
BATCH = 8
SEQ = 2048
DEPTH = 4
DEC_BATCH = 128
DEC_SEQ = 4
PAST_LEN = 16384
PAGE_SIZE = 128

HEAD_DIM = 64
N_HEADS_A = 8
N_HEADS_R = 8
N_HEADS_M = 8
A_W = N_HEADS_A * HEAD_DIM
R_W = N_HEADS_R * HEAD_DIM
M_W = N_HEADS_M * HEAD_DIM
BR_W = A_W
N_BRANCH = 3
W_LORA = 64
A_LORA = 64
G_LORA = 128
CONV_W = 4
D_FF = 4 * D_MODEL
CHUNK = 64
ROPE_BASE = 10000.0
RMS_EPS = 1e-6
RWKV_GN_EPS = 64e-5
GN_EPS = 1e-5

RWKV_PROJ = 3 * A_W + W_LORA + A_LORA + G_LORA
RET_PROJ = 4 * R_W
MLSTM_PROJ = 4 * M_W + 2 * N_HEADS_M
GATE_PROJ = N_BRANCH * D_MODEL
IN_PROJ = RWKV_PROJ + RET_PROJ + MLSTM_PROJ + GATE_PROJ
IN_SPLITS = (RWKV_PROJ, RWKV_PROJ + RET_PROJ, RWKV_PROJ + RET_PROJ + MLSTM_PROJ)
RWKV_SPLITS = (A_W, 2 * A_W, 3 * A_W, 3 * A_W + W_LORA, 3 * A_W + W_LORA + A_LORA)
MLSTM_SPLITS = (2 * M_W, 3 * M_W, 4 * M_W, 4 * M_W + N_HEADS_M)
N_STATES = 7

kernel_name = 'hybrid_rwkv7_retnet_mlstm_step'

F32 = jnp.float32


def rmsnorm(x, g):
    x32 = x.astype(F32)
    y = x32 * lax.rsqrt(jnp.mean(x32 * x32, axis=-1, keepdims=True) + RMS_EPS)
    return (y * g.astype(F32)).astype(x.dtype)


def head_norm(h, n_heads, eps):
    hs = h.reshape(h.shape[:-1] + (n_heads, HEAD_DIM))
    mu = jnp.mean(hs, axis=-1, keepdims=True)
    var = jnp.mean(jnp.square(hs - mu), axis=-1, keepdims=True)
    return ((hs - mu) * lax.rsqrt(var + eps)).reshape(h.shape)


def to_chunks(t, L):
    B, T = t.shape[0], t.shape[1]
    return jnp.moveaxis(t.reshape((B, T // L, L) + t.shape[2:]), 1, 0)


def from_chunks(t):
    t = jnp.moveaxis(t, 0, 1)
    return t.reshape((t.shape[0], t.shape[1] * t.shape[2]) + t.shape[3:])


def rotary(x, pos):
    half = HEAD_DIM // 2
    inv = jnp.power(ROPE_BASE, -jnp.arange(half, dtype=F32) / half)
    ang = pos[:, None] * inv[None, :]
    cos = jnp.cos(ang)[None, :, None, :]
    sin = jnp.sin(ang)[None, :, None, :]
    x1, x2 = x[..., :half], x[..., half:]
    return jnp.concatenate([x1 * cos - x2 * sin, x1 * sin + x2 * cos], axis=-1)


def rwkv7_mixer(pa, shift_prev, s0, lp):
    B, T, _ = pa.shape
    pa = pa.astype(F32)
    prev = jnp.concatenate([shift_prev.astype(F32)[:, None], pa[:, :-1]], axis=1)
    pm = pa + (prev - pa) * lp['rwkv_mu']
    r, k, v, wd, ad, gd = jnp.split(pm, RWKV_SPLITS, axis=-1)
    w_log = -jax.nn.softplus(-(lp['rwkv_w0'] + jnp.tanh(wd) @ lp['rwkv_w2'])) - 0.5
    decay = jnp.exp(-jnp.exp(w_log))
    a = jax.nn.sigmoid(lp['rwkv_a0'] + ad @ lp['rwkv_a2'])
    g = jax.nn.sigmoid(gd) @ lp['rwkv_g2']
    heads = lambda t: t.reshape(B, T, N_HEADS_A, HEAD_DIM)
    kk = heads(k * lp['rwkv_kk'])
    kk = kk / jnp.maximum(jnp.sqrt(jnp.sum(kk * kk, axis=-1, keepdims=True)), 1e-12)
    k = k * (1.0 + (a - 1.0) * lp['rwkv_ka'])
    rh, kh, vh, ah, dh = heads(r), heads(k), heads(v), heads(a), heads(decay)

    def step(S, xs):
        r_t, w_t, k_t, v_t, kk_t, a_t = xs
        sk = jnp.einsum('bhvk,bhk->bhv', S, kk_t)
        S = (S * w_t[:, :, None, :] - sk[..., None] * (kk_t * a_t)[:, :, None, :]
             + v_t[..., None] * k_t[:, :, None, :])
        return S, jnp.einsum('bhvk,bhk->bhv', S, r_t)

    xs = tuple(jnp.moveaxis(t, 1, 0) for t in (rh, dh, kh, vh, kk, ah))
    s_end, out = lax.scan(step, s0.astype(F32), xs)
    out = jnp.moveaxis(out, 0, 1).reshape(B, T, A_W)
    o = head_norm(out, N_HEADS_A, RWKV_GN_EPS) * lp['rwkv_ln_g'] + lp['rwkv_ln_b']
    bonus = jnp.sum(rh * kh * lp['rwkv_rk'], axis=-1, keepdims=True) * vh
    o = (o + bonus.reshape(B, T, A_W)) * g
    return o, pa[:, -1], s_end


def retention_mixer(pr, s0, pos0, lp):
    B, T, _ = pr.shape
    q, k, v, z = jnp.split(pr.astype(F32), 4, axis=-1)
    heads = lambda t: t.reshape(B, T, N_HEADS_R, HEAD_DIM)
    pos = jnp.arange(T, dtype=F32) + pos0
    q = rotary(heads(q), pos)
    k = rotary(heads(k), pos) * (HEAD_DIM ** -0.5)
    v = heads(v)
    L = math.gcd(T, CHUNK)
    log_g = jnp.log(1.0 - jnp.exp2(-5.0 - jnp.arange(N_HEADS_R, dtype=F32)))
    idx = jnp.arange(L, dtype=F32)
    diff = idx[:, None] - idx[None, :]
    dmask = jnp.where(diff[None] >= 0, jnp.exp(jnp.maximum(diff, 0.0)[None] * log_g[:, None, None]), 0.0)
    q_in = jnp.exp((idx + 1.0)[:, None] * log_g[None, :])
    k_end = jnp.exp((L - 1.0 - idx)[:, None] * log_g[None, :])
    g_chunk = jnp.exp(L * log_g)

    def chunk(S, xs):
        qc, kc, vc = xs
        sc = jnp.einsum('bihd,bjhd->bhij', qc, kc) * dmask
        o = (jnp.einsum('bhij,bjhe->bihe', sc, vc)
             + jnp.einsum('bihd,bhde->bihe', qc, S) * q_in[None, :, :, None])
        S = S * g_chunk[None, :, None, None] + jnp.einsum('bjhd,bjhe->bhde', kc * k_end[None, :, :, None], vc)
        return S, o

    s_end, o = lax.scan(chunk, s0.astype(F32), (to_chunks(q, L), to_chunks(k, L), to_chunks(v, L)))
    o = from_chunks(o).reshape(B, T, R_W)
    o = head_norm(o, N_HEADS_R, GN_EPS) * lp['ret_gn_g']
    return jax.nn.silu(z) * o, s_end


def mlstm_mixer(pm, conv_prev, c0, n0, m0, lp):
    B, T, _ = pm.shape
    pm = pm.astype(F32)
    qk_pre, v, o_pre, i_pre, f_pre = jnp.split(pm, MLSTM_SPLITS, axis=-1)
    xc = jnp.concatenate([conv_prev.astype(F32), qk_pre], axis=1)
    conv = lp['mlstm_conv_b'] + sum(xc[:, w:w + T] * lp['mlstm_conv_w'][w] for w in range(CONV_W))
    new_conv = xc[:, T:]
    q, k = jnp.split(jax.nn.silu(conv), 2, axis=-1)
    heads = lambda t: t.reshape(B, T, N_HEADS_M, HEAD_DIM)
    q, k, v = heads(q), heads(k) * (HEAD_DIM ** -0.5), heads(v)
    ig = i_pre + lp['mlstm_b_i']
    lf = jax.nn.log_sigmoid(f_pre + lp['mlstm_b_f'])
    L = math.gcd(T, CHUNK)
    tri = jnp.tril(jnp.ones((L, L), dtype=bool))

    def chunk(carry, xs):
        C, n, m = carry
        qc, kc, vc, igc, lfc = xs
        b = jnp.cumsum(lfc, axis=1)
        u = igc - b
        mt = b + jnp.maximum(m[:, None, :], lax.cummax(u, axis=1))
        logd = (b - mt)[:, :, None, :] + u[:, None, :, :]
        dmat = jnp.where(tri[None, :, :, None], jnp.exp(jnp.minimum(logd, 0.0)), 0.0)
        inter = jnp.exp(b + m[:, None, :] - mt)
        sc = jnp.einsum('bthd,bjhd->btjh', qc, kc) * dmat
        num = (jnp.einsum('btjh,bjhe->bthe', sc, vc)
               + inter[..., None] * jnp.einsum('bthd,bhde->bthe', qc, C))
        den = jnp.sum(sc, axis=2) + inter * jnp.einsum('bthd,bhd->bth', qc, n)
        h = num / jnp.maximum(jnp.abs(den), jnp.exp(-mt))[..., None]
        m_end = mt[:, -1]
        w_end = jnp.exp(b[:, -1:] - m_end[:, None] + u)
        s_end = inter[:, -1]
        C = s_end[:, :, None, None] * C + jnp.einsum('bjh,bjhd,bjhe->bhde', w_end, kc, vc)
        n = s_end[..., None] * n + jnp.einsum('bjh,bjhd->bhd', w_end, kc)
        return (C, n, m_end), h

    xs = (to_chunks(q, L), to_chunks(k, L), to_chunks(v, L), to_chunks(ig, L), to_chunks(lf, L))
    (c_end, n_end, m_end), h = lax.scan(chunk, (c0.astype(F32), n0.astype(F32), m0.astype(F32)), xs)
    h = from_chunks(h).reshape(B, T, M_W)
    h = jax.nn.sigmoid(o_pre) * h
    h = head_norm(h, N_HEADS_M, GN_EPS) * lp['mlstm_gn_g']
    return h, c_end, n_end, m_end, new_conv


def trunk_layer(x, lp, st, pos0):
    B, T, _ = x.shape
    xn = rmsnorm(x, lp['norm_mix'])
    proj = xn @ lp['w_in']
    pa, pr, pm, pg = jnp.split(proj, IN_SPLITS, axis=-1)
    ha, shift_new, wkv_new = rwkv7_mixer(pa, st[1], st[0], lp)
    hr, ret_new = retention_mixer(pr, st[2], pos0, lp)
    hm, c_new, n_new, m_new, conv_new = mlstm_mixer(pm, st[6], st[3], st[4], st[5], lp)
    hb = jnp.stack([ha, hr, hm], axis=2).astype(x.dtype)
    ub = jnp.einsum('btnc,ncd->btnd', hb, lp['w_branch'])
    gate = jax.nn.sigmoid(pg.reshape(B, T, N_BRANCH, D_MODEL) + lp['b_gate'])
    mixed = jnp.sum(gate * ub, axis=2)
    x = x + (mixed @ lp['w_out']).astype(x.dtype)
    hff = rmsnorm(x, lp['norm_ffn']) @ lp['w_ff1']
    x = x + (jnp.square(jax.nn.relu(hff)) @ lp['w_ff2']).astype(x.dtype)
    return x, (wkv_new, shift_new, ret_new, c_new, n_new, m_new, conv_new)


def run_trunk(x, states, layer_params, norm_final, pos0):
    new = [[] for _ in range(N_STATES)]
    for l in range(DEPTH):
        lp = {name: arr[l] for name, arr in layer_params.items()}
        x, st = trunk_layer(x, lp, tuple(s[l] for s in states), pos0)
        for i in range(N_STATES):
            new[i].append(st[i])
    return rmsnorm(x, norm_final), tuple(jnp.stack(s, axis=0) for s in new)


def zero_states(B):
    return (jnp.zeros((DEPTH, B, N_HEADS_A, HEAD_DIM, HEAD_DIM), F32),
            jnp.zeros((DEPTH, B, RWKV_PROJ), F32),
            jnp.zeros((DEPTH, B, N_HEADS_R, HEAD_DIM, HEAD_DIM), F32),
            jnp.zeros((DEPTH, B, N_HEADS_M, HEAD_DIM, HEAD_DIM), F32),
            jnp.zeros((DEPTH, B, N_HEADS_M, HEAD_DIM), F32),
            jnp.zeros((DEPTH, B, N_HEADS_M), F32),
            jnp.zeros((DEPTH, B, CONV_W - 1, 2 * M_W), F32))


def setup_inputs(seed: int = 0) -> dict:
    key = jax.random.key(seed)
    keys = jax.random.split(key, 40)

    def nrm(i, shape, scale):
        return scale * jax.random.normal(keys[i], shape, F32)

    w0_base = jnp.tile(jnp.linspace(-6.0, -1.0, HEAD_DIM, dtype=F32), N_HEADS_A)
    bf_base = jnp.linspace(3.0, 6.0, N_HEADS_M, dtype=F32)
    return {
        'x_prompt': nrm(0, (BATCH, SEQ, D_MODEL), 1.0),
        'x_sample': nrm(1, (DEC_BATCH, DEC_SEQ, D_MODEL), 1.0),
        'state_rwkv_wkv': nrm(2, (DEPTH, DEC_BATCH, N_HEADS_A, HEAD_DIM, HEAD_DIM), 0.3),
        'state_rwkv_shift': nrm(3, (DEPTH, DEC_BATCH, RWKV_PROJ), 1.0),
        'state_ret': nrm(4, (DEPTH, DEC_BATCH, N_HEADS_R, HEAD_DIM, HEAD_DIM), 0.5),
        'state_mlstm_C': nrm(5, (DEPTH, DEC_BATCH, N_HEADS_M, HEAD_DIM, HEAD_DIM), 0.3),
        'state_mlstm_n': nrm(6, (DEPTH, DEC_BATCH, N_HEADS_M, HEAD_DIM), 0.3),
        'state_mlstm_m': nrm(7, (DEPTH, DEC_BATCH, N_HEADS_M), 1.0),
        'state_mlstm_conv': nrm(8, (DEPTH, DEC_BATCH, CONV_W - 1, 2 * M_W), 1.0),
        'norm_mix': 1.0 + nrm(9, (DEPTH, D_MODEL), 0.02),
        'w_in': nrm(10, (DEPTH, D_MODEL, IN_PROJ), D_MODEL ** -0.5),
        'rwkv_mu': jax.random.uniform(keys[11], (DEPTH, RWKV_PROJ), F32),
        'rwkv_w0': w0_base + nrm(12, (DEPTH, A_W), 0.1),
        'rwkv_w2': nrm(13, (DEPTH, W_LORA, A_W), 0.1 * W_LORA ** -0.5),
        'rwkv_a0': nrm(14, (DEPTH, A_W), 0.1),
        'rwkv_a2': nrm(15, (DEPTH, A_LORA, A_W), 0.1 * A_LORA ** -0.5),
        'rwkv_g2': nrm(16, (DEPTH, G_LORA, A_W), G_LORA ** -0.5),
        'rwkv_kk': 0.85 + nrm(17, (DEPTH, A_W), 0.05),
        'rwkv_ka': 1.0 + nrm(18, (DEPTH, A_W), 0.05),
        'rwkv_rk': nrm(19, (DEPTH, N_HEADS_A, HEAD_DIM), 0.1),
        'rwkv_ln_g': 1.0 + nrm(20, (DEPTH, A_W), 0.02),
        'rwkv_ln_b': nrm(21, (DEPTH, A_W), 0.02),
        'ret_gn_g': 1.0 + nrm(22, (DEPTH, R_W), 0.02),
        'mlstm_conv_w': nrm(23, (DEPTH, CONV_W, 2 * M_W), CONV_W ** -0.5),
        'mlstm_conv_b': nrm(24, (DEPTH, 2 * M_W), 0.02),
        'mlstm_b_i': nrm(25, (DEPTH, N_HEADS_M), 0.1),
        'mlstm_b_f': bf_base + nrm(26, (DEPTH, N_HEADS_M), 0.1),
        'mlstm_gn_g': 1.0 + nrm(27, (DEPTH, M_W), 0.02),
        'w_branch': nrm(28, (DEPTH, N_BRANCH, BR_W, D_MODEL), BR_W ** -0.5),
        'b_gate': nrm(29, (DEPTH, N_BRANCH, D_MODEL), 0.02),
        'w_out': nrm(30, (DEPTH, D_MODEL, D_MODEL), D_MODEL ** -0.5),
        'norm_ffn': 1.0 + nrm(31, (DEPTH, D_MODEL), 0.02),
        'w_ff1': nrm(32, (DEPTH, D_MODEL, D_FF), D_MODEL ** -0.5),
        'w_ff2': nrm(33, (DEPTH, D_FF, D_MODEL), 0.5 * D_FF ** -0.5),
        'norm_final': 1.0 + nrm(34, (D_MODEL,), 0.02),
    }


def reference(x_prompt, x_sample, state_rwkv_wkv, state_rwkv_shift, state_ret, state_mlstm_C, state_mlstm_n,
              state_mlstm_m, state_mlstm_conv, norm_mix, w_in, rwkv_mu, rwkv_w0, rwkv_w2, rwkv_a0, rwkv_a2,
              rwkv_g2, rwkv_kk, rwkv_ka, rwkv_rk, rwkv_ln_g, rwkv_ln_b, ret_gn_g, mlstm_conv_w, mlstm_conv_b,
              mlstm_b_i, mlstm_b_f, mlstm_gn_g, w_branch, b_gate, w_out, norm_ffn, w_ff1, w_ff2, norm_final):
    layer_params = dict(norm_mix=norm_mix, w_in=w_in, rwkv_mu=rwkv_mu, rwkv_w0=rwkv_w0, rwkv_w2=rwkv_w2,
                        rwkv_a0=rwkv_a0, rwkv_a2=rwkv_a2, rwkv_g2=rwkv_g2, rwkv_kk=rwkv_kk, rwkv_ka=rwkv_ka,
                        rwkv_rk=rwkv_rk, rwkv_ln_g=rwkv_ln_g, rwkv_ln_b=rwkv_ln_b, ret_gn_g=ret_gn_g,
                        mlstm_conv_w=mlstm_conv_w, mlstm_conv_b=mlstm_conv_b, mlstm_b_i=mlstm_b_i,
                        mlstm_b_f=mlstm_b_f, mlstm_gn_g=mlstm_gn_g, w_branch=w_branch, b_gate=b_gate,
                        w_out=w_out, norm_ffn=norm_ffn, w_ff1=w_ff1, w_ff2=w_ff2)
    y_prompt, p_states = run_trunk(x_prompt, zero_states(x_prompt.shape[0]), layer_params, norm_final, 0)
    s_in = (state_rwkv_wkv, state_rwkv_shift, state_ret, state_mlstm_C, state_mlstm_n, state_mlstm_m,
            state_mlstm_conv)
    y_sample, s_states = run_trunk(x_sample, s_in, layer_params, norm_final, PAST_LEN)
    p_wkv, p_shift, p_ret, p_c, p_n, p_m, p_conv = p_states
    s_wkv, s_shift, s_ret, s_c, s_n, s_m, s_conv = s_states
    return (y_prompt, y_sample, p_wkv, p_shift, p_ret, p_c, p_n, p_m, p_conv,
            s_wkv, s_shift, s_ret, s_c, s_n, s_m, s_conv)
```

```python
import functools
import math

import jax
import jax.numpy as jnp
from jax import lax
from jax.experimental import pallas as pl
from jax.experimental.pallas import tpu as pltpu

F32 = jnp.float32
BF16 = jnp.bfloat16

D_MODEL = 1024
DEPTH = 4
HEAD_DIM = 64
N_HEADS = 8
BR_W = N_HEADS * HEAD_DIM
W_LORA = 64
A_LORA = 64
G_LORA = 128
CONV_W = 4
D_FF = 4 * D_MODEL
ROPE_BASE = 10000.0
RMS_EPS = 1e-6
RWKV_GN_EPS = 64e-5
GN_EPS = 1e-5
RWKV_PROJ = 3 * BR_W + W_LORA + A_LORA + G_LORA

COL_RET = 0
COL_ML = 2048
COL_RW = 4096
COL_IF = COL_RW + RWKV_PROJ
COL_GATE = 6144
N_PROJ = COL_GATE + 3 * D_MODEL

PROMPT_CHUNK = 128
SAMPLE_SEQS = 8
SCAN_BATCH = 8
SCAN_TBLK = 64
VMEM_LIMIT = 56 * 1024 * 1024


def _cparams(sem):
    return pltpu.CompilerParams(dimension_semantics=sem, vmem_limit_bytes=VMEM_LIMIT)


def _dot(a, b):
    return jnp.dot(a, b, preferred_element_type=F32)


def _dot_nt(a, b):
    return lax.dot_general(a, b, (((1,), (1,)), ((), ())), preferred_element_type=F32)


def _dot_tn(a, b):
    return lax.dot_general(a, b, (((0,), (0,)), ((), ())), preferred_element_type=F32)


def _split2(x):
    hi = x.astype(BF16)
    lo = (x - hi.astype(F32)).astype(BF16)
    return hi, lo


def _split3(x):
    x1 = x.astype(BF16)
    r1 = x - x1.astype(F32)
    x2 = r1.astype(BF16)
    x3 = (r1 - x2.astype(F32)).astype(BF16)
    return x1, x2, x3


def _sigmoid(x):
    return 1.0 / (1.0 + jnp.exp(-x))


def _softplus(x):
    return jnp.maximum(x, 0.0) + jnp.log(1.0 + jnp.exp(-jnp.abs(x)))


def _seg_bf16(xb, bo):
    return jnp.concatenate([_dot(xb[:, :256], bo), _dot(xb[:, 256:], bo)], axis=1)


def _segsum(x, bo):
    hi, lo = _split2(x)
    return _seg_bf16(hi, bo) + _seg_bf16(lo, bo)


def _head_norm(h, bo, eps):
    mu = _segsum(h, bo) * (1.0 / HEAD_DIM)
    d = h - mu
    var = _segsum(d * d, bo) * (1.0 / HEAD_DIM)
    return d * lax.rsqrt(var + eps)


def _rms(x, g):
    return x * lax.rsqrt(jnp.mean(x * x, axis=-1, keepdims=True) + RMS_EPS) * g


def _inproj_kernel(x_ref, g_ref, w_ref, o_ref, xn_scr):
    @pl.when(pl.program_id(1) == 0)
    def _():
        xn_scr[...] = _rms(x_ref[...], g_ref[...]).astype(BF16)

    o_ref[...] = _dot(xn_scr[...], w_ref[...])


def _inproj(x, g, w, tm, tn):
    R = x.shape[0]
    return pl.pallas_call(
        _inproj_kernel,
        grid=(R // tm, N_PROJ // tn),
        in_specs=[pl.BlockSpec((tm, D_MODEL), lambda i, j: (i, 0)),
                  pl.BlockSpec((1, D_MODEL), lambda i, j: (0, 0)),
                  pl.BlockSpec((D_MODEL, tn), lambda i, j: (0, j))],
        out_specs=pl.BlockSpec((tm, tn), lambda i, j: (i, j)),
        out_shape=jax.ShapeDtypeStruct((R, N_PROJ), F32),
        scratch_shapes=[pltpu.VMEM((tm, D_MODEL), BF16)],
        compiler_params=_cparams(("parallel", "arbitrary")),
        name="inproj",
    )(x, g, w)


def _rwkv_prep_kernel(p_ref, p8_ref, fix_ref, mu_ref, w0_ref, w2_ref, a0_ref, a2_ref, g2_ref, kkp_ref, kap_ref,
                      rk_ref, bo_ref, r_ref, w_ref, kt_ref, nb_ref, kk_ref, v_ref, g_ref, bon_ref,
                      *, tm, n_prompt_tiles, t_prompt, t_sample):
    i = pl.program_id(0)
    pa = p_ref[:, :RWKV_PROJ]
    row = lax.broadcasted_iota(jnp.int32, (tm, 1), 0)
    grow = row + i * tm
    boundary = (grow & jnp.where(i >= n_prompt_tiles, t_sample - 1, t_prompt - 1)) == 0
    prev = pltpu.roll(pa, 1, axis=0)
    prev = jnp.where(row == 0, p8_ref[7:8, :RWKV_PROJ], prev)
    prev = jnp.where(boundary, fix_ref[...], prev)
    pm = pa + (prev - pa) * mu_ref[...]
    r = pm[:, 0:512]
    k = pm[:, 512:1024]
    v = pm[:, 1024:1536]
    wa = pm[:, 1536:1664]
    gd = pm[:, 1664:1792]
    bo = bo_ref[...]
    lw = _dot(jnp.tanh(wa).astype(BF16), w2_ref[...])
    w_log = -_softplus(-(w0_ref[...] + lw)) - 0.5
    decay = jnp.exp(-jnp.exp(w_log))
    a = _sigmoid(a0_ref[...] + _dot(wa.astype(BF16), a2_ref[...]))
    g = _dot(_sigmoid(gd).astype(BF16), g2_ref[...])
    kk = k * kkp_ref[...]
    nrm = jnp.sqrt(_segsum(kk * kk, bo))
    kk = kk / jnp.maximum(nrm, 1e-12)
    kt = k * (1.0 + (a - 1.0) * kap_ref[...])
    bon = _segsum(r * kt * rk_ref[...], bo) * v
    r_ref[...] = r
    w_ref[...] = decay
    kt_ref[...] = kt
    nb_ref[...] = -(kk * a)
    kk_ref[...] = kk
    v_ref[...] = v
    g_ref[...] = g
    bon_ref[...] = bon


def _rwkv_prep(P, fix, lp, bo, tm, n_prompt_tiles, t_prompt, t_sample):
    R = P.shape[0]
    row = lambda n: pl.BlockSpec((1, n), lambda i: (0, 0))
    full = lambda a, b: pl.BlockSpec((a, b), lambda i: (0, 0))
    out = pl.BlockSpec((tm, BR_W), lambda i: (i, 0))
    kern = functools.partial(_rwkv_prep_kernel, tm=tm, n_prompt_tiles=n_prompt_tiles, t_prompt=t_prompt,
                             t_sample=t_sample)
    return pl.pallas_call(
        kern,
        grid=(R // tm,),
        in_specs=[pl.BlockSpec((tm, 2048), lambda i: (i, COL_RW // 2048)),
                  pl.BlockSpec((8, 2048), lambda i: (jnp.maximum(i * (tm // 8) - 1, 0), COL_RW // 2048)),
                  pl.BlockSpec((tm, RWKV_PROJ), lambda i: (jnp.maximum(i - n_prompt_tiles + 1, 0), 0)),
                  row(RWKV_PROJ), row(BR_W), full(128, BR_W), row(BR_W), full(128, BR_W), full(128, BR_W),
                  row(BR_W), row(BR_W), row(BR_W), full(256, 256)],
        out_specs=[out] * 8,
        out_shape=[jax.ShapeDtypeStruct((R, BR_W), F32)] * 8,
        compiler_params=_cparams(("parallel",)),
        name="rwkv_prep",
    )(P, P, fix, lp['mu'], lp['w0'], lp['w2'], lp['a0'], lp['a2'], lp['g2'], lp['kk'], lp['ka'], lp['rk'], bo)


def _rwkv_scan_kernel(r_ref, w_ref, kt_ref, nb_ref, kk_ref, v_ref, g_ref, bon_ref, s0_ref, bo_ref, lng_ref,
                      lnb_ref, ha_ref, sT_ref, S_scr, o_scr, *, nb, tb):
    j = pl.program_id(1)

    @pl.when(j == 0)
    def _():
        S_scr[...] = s0_ref[...]

    bo = bo_ref[...]
    lane = lax.broadcasted_iota(jnp.int32, (HEAD_DIM, BR_W), 1)
    sub = lax.broadcasted_iota(jnp.int32, (HEAD_DIM, BR_W), 0)
    sel = (lane & (HEAD_DIM - 1)) == sub

    def step(t, carry):
        for b in range(nb):
            row = lambda ref: ref[b, pl.ds(t, 1), :]
            S = S_scr[b]
            sk = _seg_bf16((S * row(kk_ref)).astype(BF16), bo)
            v_t = row(v_ref)
            v_hi = v_t.astype(BF16).astype(F32)
            v_lo = v_t - v_hi
            vc = (_seg_bf16(jnp.where(sel, v_hi, 0.0).astype(BF16), bo)
                  + _seg_bf16(jnp.where(sel, v_lo, 0.0).astype(BF16), bo))
            Sn = S * row(w_ref) + sk * row(nb_ref) + vc * row(kt_ref)
            S_scr[b] = Sn
            G = _seg_bf16((Sn * row(r_ref)).astype(BF16), bo)
            o_scr[b, pl.ds(t, 1), :] = jnp.sum(jnp.where(sel, G, 0.0), axis=0, keepdims=True)
        return carry

    lax.fori_loop(0, tb, step, 0)

    for b in range(nb):
        o = _head_norm(o_scr[b], bo, RWKV_GN_EPS) * lng_ref[...] + lnb_ref[...]
        ha_ref[b] = (o + bon_ref[b]) * g_ref[b]

    @pl.when(j == pl.num_programs(1) - 1)
    def _():
        sT_ref[...] = S_scr[...]


def _rwkv_scan(seqs, s0, bo, ln_g, ln_b, tb):
    B, T, _ = seqs[0].shape
    nb = SCAN_BATCH
    blk = pl.BlockSpec((nb, tb, BR_W), lambda i, j: (i, j, 0))
    st = pl.BlockSpec((nb, HEAD_DIM, BR_W), lambda i, j: (i, 0, 0))
    kern = functools.partial(_rwkv_scan_kernel, nb=nb, tb=tb)
    return pl.pallas_call(
        kern,
        grid=(B // nb, T // tb),
        in_specs=[blk] * 8 + [st, pl.BlockSpec((256, 256), lambda i, j: (0, 0)),
                              pl.BlockSpec((1, BR_W), lambda i, j: (0, 0)),
                              pl.BlockSpec((1, BR_W), lambda i, j: (0, 0))],
        out_specs=[blk, st],
        out_shape=[jax.ShapeDtypeStruct((B, T, BR_W), F32), jax.ShapeDtypeStruct((B, HEAD_DIM, BR_W), F32)],
        scratch_shapes=[pltpu.VMEM((nb, HEAD_DIM, BR_W), F32), pltpu.VMEM((nb, tb, BR_W), F32)],
        compiler_params=_cparams(("parallel", "arbitrary")),
        name="rwkv_scan",
    )(*seqs, s0, bo, ln_g, ln_b)


def _pair_masks():
    lane = lax.broadcasted_iota(jnp.int32, (1, 128), 1)
    m0 = (lane < HEAD_DIM).astype(F32)
    return m0, 1.0 - m0


def _swap_halves(x):
    lane = lax.broadcasted_iota(jnp.int32, x.shape, 1)
    n = x.shape[1]
    return jnp.where((lane & 63) < 32, pltpu.roll(x, n - 32, axis=1), pltpu.roll(x, 32, axis=1))


def _ret_kernel(q_ref, k_ref, v_ref, z_ref, cos_ref, sin_ref, dm_ref, qin_ref, kend_ref, gs_ref, bd_ref, s0_ref,
                bo_ref, gn_ref, h_ref, sT_ref, S_scr, *, nseq, L):
    c = pl.program_id(1)

    @pl.when(c == 0)
    def _():
        S_scr[...] = s0_ref[...]

    cos = cos_ref[...]
    sin = sin_ref[...]
    q = q_ref[...]
    k = k_ref[...]
    q = q * cos + _swap_halves(q) * sin
    k = (k * cos + _swap_halves(k) * sin) * (HEAD_DIM ** -0.5)
    v = v_ref[...]
    qin = qin_ref[...]
    ke = k * kend_ref[...]
    m0, m1 = _pair_masks()
    rows = nseq * L
    rowi = lax.broadcasted_iota(jnp.int32, (rows, 1), 0)
    outs = []
    for p in range(N_HEADS // 2):
        sl = slice(128 * p, 128 * (p + 1))
        qp, kp, vp, kep = q[:, sl], k[:, sl], v[:, sl], ke[:, sl]
        kb = kp.astype(BF16)
        o = jnp.zeros((rows, 128), F32)
        for e, me in enumerate((m0, m1)):
            sc = _dot_nt((qp * me).astype(BF16), kb) * dm_ref[2 * p + e]
            o = o + _dot(sc.astype(BF16), (vp * me).astype(BF16))
        qb = qp.astype(BF16)
        for s in range(nseq):
            S = S_scr[s, p]
            oi = _dot(qb, S.astype(BF16)) * qin[:, sl]
            if nseq == 1:
                o = o + oi
                kes = kep
            else:
                inseq = (rowi >= s * L) & (rowi < (s + 1) * L)
                o = o + jnp.where(inseq, oi, 0.0)
                kes = jnp.where(inseq, kep, 0.0)
            S_scr[s, p] = S * gs_ref[p] + bd_ref[...] * _dot_tn(kes.astype(BF16), vp.astype(BF16))
        outs.append(o)
    o = jnp.concatenate(outs, axis=1)
    o = _head_norm(o, bo_ref[...], GN_EPS) * gn_ref[...]
    z = z_ref[...]
    h_ref[...] = z * _sigmoid(z) * o

    @pl.when(c == pl.num_programs(1) - 1)
    def _():
        sT_ref[...] = S_scr[...]


def _retention(P, row0, n_groups, n_chunks, nseq, L, tabs, s0, bo, gn):
    rows = nseq * L
    rb0 = row0 // rows
    cos, sin, dm, qin, kend, gs, bd = tabs
    col = lambda cb: pl.BlockSpec((rows, BR_W), lambda i, c: (rb0 + i * n_chunks + c, cb))
    tab = pl.BlockSpec((rows, BR_W), lambda i, c: (c, 0))
    cst = lambda shp: pl.BlockSpec(shp, lambda i, c: (0,) * len(shp))
    st = pl.BlockSpec((nseq, 4, 128, 128), lambda i, c: (i, 0, 0, 0))
    kern = functools.partial(_ret_kernel, nseq=nseq, L=L)
    return pl.pallas_call(
        kern,
        grid=(n_groups, n_chunks),
        in_specs=[col(0), col(1), col(2), col(3), tab, tab, cst((N_HEADS, rows, rows)), cst((rows, BR_W)),
                  cst((rows, BR_W)), cst((4, 128, 128)), cst((128, 128)), st, cst((256, 256)), cst((1, BR_W))],
        out_specs=[pl.BlockSpec((rows, BR_W), lambda i, c: (i * n_chunks + c, 0)), st],
        out_shape=[jax.ShapeDtypeStruct((n_groups * n_chunks * rows, BR_W), F32),
                   jax.ShapeDtypeStruct((n_groups * nseq, 4, 128, 128), F32)],
        scratch_shapes=[pltpu.VMEM((nseq, 4, 128, 128), F32)],
        compiler_params=_cparams(("parallel", "arbitrary")),
        name="retention",
    )(P, P, P, P, cos, sin, dm, qin, kend, gs, bd, s0, bo, gn)


def _mlstm_kernel(qk_ref, v_ref, o_ref, if_ref, xcs_ref, cw_ref, cb_ref, bi_ref, bf_ref, tri_ref, bd_ref, eye_ref,
                  c0_ref, n0_ref, m0_ref, bo_ref, gn_ref, h_ref, cT_ref, nT_ref, mT_ref,
                  C_scr, n_scr, m_scr, xc_scr, *, nseq, L):
    c = pl.program_id(1)
    rows = nseq * L

    @pl.when(c == 0)
    def _():
        C_scr[...] = c0_ref[...]
        n_scr[...] = n0_ref[...]
        m_scr[...] = m0_ref[...]
        if nseq == 1:
            xc_scr[0:8, :] = jnp.zeros((8, 2 * BR_W), F32)

    cw = cw_ref[...]
    if nseq == 1:
        xc_scr[8:8 + L, :] = qk_ref[...]
        conv = cb_ref[...]
        for w in range(CONV_W):
            conv = conv + xc_scr[5 + w:5 + w + L, :] * cw[w:w + 1, :]
        xc_scr[0:8, :] = xc_scr[L:L + 8, :]
    else:
        parts = []
        for s in range(nseq):
            cs = cb_ref[...]
            for w in range(CONV_W):
                cs = cs + xcs_ref[s, 1 + w:1 + w + L, :] * cw[w:w + 1, :]
            parts.append(cs)
        conv = jnp.concatenate(parts, axis=0)
    qk = conv * _sigmoid(conv)
    q = qk[:, :BR_W]
    k = qk[:, BR_W:] * (HEAD_DIM ** -0.5)
    v = v_ref[...]

    rowi = lax.broadcasted_iota(jnp.int32, (rows, 1), 0)
    tok = rowi % L
    ig = if_ref[:, 0:128] + bi_ref[...]
    fp = if_ref[:, 128:256] + bf_ref[...]
    lf = jnp.minimum(fp, 0.0) - jnp.log(1.0 + jnp.exp(-jnp.abs(fp)))
    tri = tri_ref[...]
    l1, l2, l3 = _split3(lf)
    bcum = _dot(tri, l1) + _dot(tri, l2) + _dot(tri, l3)
    u = ig - bcum
    cm = u
    sh = 1
    while sh < L:
        cm = jnp.maximum(cm, jnp.where(tok >= sh, pltpu.roll(cm, sh, axis=0), -jnp.inf))
        sh *= 2
    if nseq == 1:
        m0r = m_scr[0]
    else:
        m0r = jnp.concatenate([jnp.broadcast_to(m_scr[s], (L, 128)) for s in range(nseq)], axis=0)
    mt = bcum + jnp.maximum(m0r, cm)
    inter = jnp.exp(bcum + m0r - mt)
    bmm = bcum - mt
    emt = jnp.exp(-mt)
    wend_parts = []
    for s in range(nseq):
        last = slice(s * L + L - 1, s * L + L)
        m_end = mt[last, :]
        wend_parts.append(jnp.exp(bcum[last, :] - m_end + u[s * L:(s + 1) * L, :]))
    wend = wend_parts[0] if nseq == 1 else jnp.concatenate(wend_parts, axis=0)
    eye = eye_ref[...]
    u1, u2, u3 = _split3(u)
    ut = _dot_nt(eye, u1) + _dot_nt(eye, u2) + _dot_nt(eye, u3)
    trib = tri > 0.5

    m0m, m1m = _pair_masks()
    lane128 = lax.broadcasted_iota(jnp.int32, (1, 128), 1)
    first = lane128 < HEAD_DIM
    bo = bo_ref[...]
    qn_all = None
    outs = []
    wk_all = []
    for p in range(N_HEADS // 2):
        sl = slice(128 * p, 128 * (p + 1))
        qp, kp, vp = q[:, sl], k[:, sl], v[:, sl]
        kb = kp.astype(BF16)
        num = jnp.zeros((rows, 128), F32)
        dsum = []
        for e, me in enumerate((m0m, m1m)):
            h = 2 * p + e
            logd = bmm[:, h:h + 1] + ut[h:h + 1, :]
            dmat = jnp.where(trib, jnp.exp(jnp.minimum(logd, 0.0)), 0.0)
            sc = _dot_nt((qp * me).astype(BF16), kb) * dmat
            num = num + _dot(sc.astype(BF16), (vp * me).astype(BF16))
            dsum.append(jnp.sum(sc, axis=1, keepdims=True))
        pick = lambda arr: jnp.where(first, arr[:, 2 * p:2 * p + 1], arr[:, 2 * p + 1:2 * p + 2])
        inter_p = pick(inter)
        den = jnp.where(first, dsum[0], dsum[1])
        wend_p = pick(wend)
        qb = qp.astype(BF16)
        kw = kp * wend_p
        wk_all.append(kw)
        qn = jnp.zeros((rows, 128), F32)
        for s in range(nseq):
            C = C_scr[s, p]
            nrow = n_scr[s][:, sl]
            qc = _dot(qb, C.astype(BF16))
            qns = qp * nrow
            if nseq == 1:
                num = num + inter_p * qc
                qn = qns
                kws = kw
                send = inter_p[L - 1:L, :]
            else:
                inseq = (rowi >= s * L) & (rowi < (s + 1) * L)
                num = num + jnp.where(inseq, inter_p * qc, 0.0)
                qn = qn + jnp.where(inseq, qns, 0.0)
                kws = jnp.where(inseq, kw, 0.0)
                send = inter_p[s * L + L - 1:s * L + L, :]
            send_col = jnp.where(lax.broadcasted_iota(jnp.int32, (128, 1), 0) < HEAD_DIM,
                                 send[:, 0:1], send[:, 64:65])
            C_scr[s, p] = C * send_col + bd_ref[...] * _dot_tn(kws.astype(BF16), vp.astype(BF16))
        den = den + inter_p * _segsum_pair(qn, bo)
        outs.append(num / jnp.maximum(jnp.abs(den), pick(emt)))
    hh = jnp.concatenate(outs, axis=1)
    kw_full = jnp.concatenate(wk_all, axis=1)
    for s in range(nseq):
        last = slice(s * L + L - 1, s * L + L)
        send_row = _expand_heads(inter[last, :])
        n_scr[s] = send_row * n_scr[s] + jnp.sum(kw_full[s * L:(s + 1) * L, :], axis=0, keepdims=True)
        m_scr[s] = mt[last, :]
    op = o_ref[...]
    hh = _sigmoid(op) * hh
    h_ref[...] = _head_norm(hh, bo, GN_EPS) * gn_ref[...]

    @pl.when(c == pl.num_programs(1) - 1)
    def _():
        cT_ref[...] = C_scr[...]
        nT_ref[...] = n_scr[...]
        mT_ref[...] = m_scr[...]


def _segsum_pair(x, bo):
    hi, lo = _split2(x)
    b2 = bo[:128, :128]
    return _dot(hi, b2) + _dot(lo, b2)


def _expand_heads(row):
    lane = lax.broadcasted_iota(jnp.int32, (1, BR_W), 1)
    out = jnp.zeros((1, BR_W), F32)
    for h in range(N_HEADS):
        out = jnp.where((lane >> 6) == h, row[:, h:h + 1], out)
    return out


def _mlstm(P, row0, n_groups, n_chunks, nseq, L, xcs, lp, tabs, c0, n0, m0, bo):
    rows = nseq * L
    rb0 = row0 // rows
    tri, bd, eye = tabs
    rowblk = lambda width, cb: pl.BlockSpec((rows, width), lambda i, c: (rb0 + i * n_chunks + c, cb))
    cst = lambda shp: pl.BlockSpec(shp, lambda i, c: (0,) * len(shp))
    stC = pl.BlockSpec((nseq, 4, 128, 128), lambda i, c: (i, 0, 0, 0))
    stn = pl.BlockSpec((nseq, 1, BR_W), lambda i, c: (i, 0, 0))
    stm = pl.BlockSpec((nseq, 1, 128), lambda i, c: (i, 0, 0))
    if nseq == 1:
        xcs_spec = cst((1, 8, 2 * BR_W))
    else:
        xcs_spec = pl.BlockSpec((nseq, 8, 2 * BR_W), lambda i, c: (i, 0, 0))
    kern = functools.partial(_mlstm_kernel, nseq=nseq, L=L)
    nB = n_groups * nseq
    return pl.pallas_call(
        kern,
        grid=(n_groups, n_chunks),
        in_specs=[rowblk(2 * BR_W, COL_ML // (2 * BR_W)), rowblk(BR_W, (COL_ML + 1024) // BR_W),
                  rowblk(BR_W, (COL_ML + 1536) // BR_W), rowblk(256, COL_IF // 256), xcs_spec,
                  cst((CONV_W, 2 * BR_W)), cst((1, 2 * BR_W)), cst((1, 128)), cst((1, 128)),
                  cst((rows, rows)), cst((128, 128)), cst((128, 128)), stC, stn, stm, cst((256, 256)),
                  cst((1, BR_W))],
        out_specs=[pl.BlockSpec((rows, BR_W), lambda i, c: (i * n_chunks + c, 0)), stC, stn, stm],
        out_shape=[jax.ShapeDtypeStruct((n_groups * n_chunks * rows, BR_W), F32),
                   jax.ShapeDtypeStruct((nB, 4, 128, 128), F32),
                   jax.ShapeDtypeStruct((nB, 1, BR_W), F32),
                   jax.ShapeDtypeStruct((nB, 1, 128), F32)],
        scratch_shapes=[pltpu.VMEM((nseq, 4, 128, 128), F32), pltpu.VMEM((nseq, 1, BR_W), F32),
                        pltpu.VMEM((nseq, 1, 128), F32), pltpu.VMEM((L + 8, 2 * BR_W), F32)],
        compiler_params=_cparams(("parallel", "arbitrary")),
        name="mlstm",
    )(P, P, P, P, xcs, lp['conv_w'], lp['conv_b'], lp['b_i'], lp['b_f'], tri, bd, eye, c0, n0, m0, bo,
      lp['mlstm_gn'])


def _mix_kernel(x_ref, ha_ref, hr_ref, hm_ref, g0_ref, g1_ref, g2_ref, wb_ref, bg_ref, wo_ref, o_ref):
    mixed = None
    for n, (h_ref, pg_ref) in enumerate(((ha_ref, g0_ref), (hr_ref, g1_ref), (hm_ref, g2_ref))):
        ub = _dot(h_ref[...].astype(BF16), wb_ref[n])
        t = _sigmoid(pg_ref[...] + bg_ref[n]) * ub
        mixed = t if mixed is None else mixed + t
    o_ref[...] = x_ref[...] + _dot(mixed.astype(BF16), wo_ref[...])


def _mix(x, ha, hr, hm, P, wb, bg, wo, tm):
    R = x.shape[0]
    xs = pl.BlockSpec((tm, D_MODEL), lambda i: (i, 0))
    hs = pl.BlockSpec((tm, BR_W), lambda i: (i, 0))
    gs = lambda n: pl.BlockSpec((tm, D_MODEL), lambda i: (i, COL_GATE // D_MODEL + n))
    return pl.pallas_call(
        _mix_kernel,
        grid=(R // tm,),
        in_specs=[xs, hs, hs, hs, gs(0), gs(1), gs(2),
                  pl.BlockSpec((3, BR_W, D_MODEL), lambda i: (0, 0, 0)),
                  pl.BlockSpec((3, 1, D_MODEL), lambda i: (0, 0, 0)),
                  pl.BlockSpec((D_MODEL, D_MODEL), lambda i: (0, 0))],
        out_specs=xs,
        out_shape=jax.ShapeDtypeStruct((R, D_MODEL), F32),
        compiler_params=_cparams(("parallel",)),
        name="mix_out",
    )(x, ha, hr, hm, P, P, P, wb, bg, wo)


def _ffn_kernel(x_ref, g_ref, w1_ref, w2_ref, gf_ref, o_ref, xn_scr, acc_scr, *, final):
    j = pl.program_id(1)

    @pl.when(j == 0)
    def _():
        xn_scr[...] = _rms(x_ref[...], g_ref[...]).astype(BF16)
        acc_scr[...] = jnp.zeros_like(acc_scr)

    h = jnp.maximum(_dot(xn_scr[...], w1_ref[...]), 0.0)
    acc_scr[...] += _dot((h * h).astype(BF16), w2_ref[...])

    @pl.when(j == pl.num_programs(1) - 1)
    def _():
        y = x_ref[...] + acc_scr[...]
        if final:
            y = _rms(y, gf_ref[...])
        o_ref[...] = y


def _ffn(x, g, w1, w2, gf, tm, tf, final):
    R = x.shape[0]
    kern = functools.partial(_ffn_kernel, final=final)
    return pl.pallas_call(
        kern,
        grid=(R // tm, D_FF // tf),
        in_specs=[pl.BlockSpec((tm, D_MODEL), lambda i, j: (i, 0)),
                  pl.BlockSpec((1, D_MODEL), lambda i, j: (0, 0)),
                  pl.BlockSpec((D_MODEL, tf), lambda i, j: (0, j)),
                  pl.BlockSpec((tf, D_MODEL), lambda i, j: (j, 0)),
                  pl.BlockSpec((1, D_MODEL), lambda i, j: (0, 0))],
        out_specs=pl.BlockSpec((tm, D_MODEL), lambda i, j: (i, 0)),
        out_shape=jax.ShapeDtypeStruct((R, D_MODEL), F32),
        scratch_shapes=[pltpu.VMEM((tm, D_MODEL), BF16), pltpu.VMEM((tm, D_MODEL), F32)],
        compiler_params=_cparams(("parallel", "arbitrary")),
        name="ffn",
    )(x, g, w1, w2, gf)


def _rope_tables(T, pos0):
    half = HEAD_DIM // 2
    inv = jnp.power(ROPE_BASE, -jnp.arange(half, dtype=F32) / half)
    ang = (jnp.arange(T, dtype=F32) + pos0)[:, None] * inv[None, :]
    cos, sin = jnp.cos(ang), jnp.sin(ang)
    cos_h = jnp.concatenate([cos, cos], axis=1)
    sin_h = jnp.concatenate([-sin, sin], axis=1)
    return jnp.tile(cos_h, (1, N_HEADS)), jnp.tile(sin_h, (1, N_HEADS))


def _ret_tables(nseq, L):
    log_g = jnp.log(1.0 - jnp.exp2(-5.0 - jnp.arange(N_HEADS, dtype=F32)))
    idx = jnp.arange(L, dtype=F32)
    diff = idx[:, None] - idx[None, :]
    dmask = jnp.where(diff[None] >= 0, jnp.exp(jnp.maximum(diff, 0.0)[None] * log_g[:, None, None]), 0.0)
    q_in = jnp.exp((idx + 1.0)[:, None] * log_g[None, :])
    k_end = jnp.exp((L - 1.0 - idx)[:, None] * log_g[None, :])
    g_chunk = jnp.exp(L * log_g)
    eye = jnp.eye(nseq, dtype=F32)
    dm = jnp.einsum('st,hij->hsitj', eye, dmask).reshape(N_HEADS, nseq * L, nseq * L)
    qin = jnp.tile(jnp.repeat(q_in, HEAD_DIM, axis=1), (nseq, 1))
    kend = jnp.tile(jnp.repeat(k_end, HEAD_DIM, axis=1), (nseq, 1))
    gs = jnp.broadcast_to(jnp.repeat(g_chunk, HEAD_DIM).reshape(4, 128, 1), (4, 128, 128))
    return dm, qin, kend, gs


def _block_diag_ones(n):
    i = jnp.arange(n) // HEAD_DIM
    return (i[:, None] == i[None, :])


def _seq_tri(nseq, L):
    r = jnp.arange(nseq * L)
    return ((r[:, None] // L == r[None, :] // L) & (r[:, None] >= r[None, :])).astype(BF16)


def _to_pairs(S):
    B = S.shape[0]
    S = S.reshape(B, 4, 2, HEAD_DIM, HEAD_DIM)
    eye = jnp.eye(2, dtype=S.dtype)
    return jnp.einsum('bpeij,ef->bpeifj', S, eye).reshape(B, 4, 128, 128)


def _from_pairs(S):
    B = S.shape[0]
    S = S.reshape(B, 4, 2, HEAD_DIM, 2, HEAD_DIM)
    return jnp.stack([S[:, :, 0, :, 0, :], S[:, :, 1, :, 1, :]], axis=2).reshape(B, N_HEADS, HEAD_DIM, HEAD_DIM)


def _pick_tile(n, cands):
    for c in cands:
        if n % c == 0:
            return c
    raise ValueError(f"no tile for {n}")


def kernel(x_prompt, x_sample, state_rwkv_wkv, state_rwkv_shift, state_ret, state_mlstm_C, state_mlstm_n,
           state_mlstm_m, state_mlstm_conv, norm_mix, w_in, rwkv_mu, rwkv_w0, rwkv_w2, rwkv_a0, rwkv_a2,
           rwkv_g2, rwkv_kk, rwkv_ka, rwkv_rk, rwkv_ln_g, rwkv_ln_b, ret_gn_g, mlstm_conv_w, mlstm_conv_b,
           mlstm_b_i, mlstm_b_f, mlstm_gn_g, w_branch, b_gate, w_out, norm_ffn, w_ff1, w_ff2, norm_final):
    Bp, Tp, _ = x_prompt.shape
    Bs, Ts, _ = x_sample.shape
    Rp, Rs = Bp * Tp, Bs * Ts
    R = Rp + Rs
    past_len = 16384
    Lp = min(PROMPT_CHUNK, Tp)
    assert Tp % Lp == 0 and Bs % SAMPLE_SEQS == 0 and Rp % (SAMPLE_SEQS * Ts) == 0
    assert Tp & (Tp - 1) == 0 and Ts & (Ts - 1) == 0 and Tp >= CONV_W - 1 and Bp % SCAN_BATCH == 0

    x = jnp.concatenate([x_prompt.reshape(Rp, D_MODEL), x_sample.reshape(Rs, D_MODEL)], axis=0)

    o_rw, o_ret, o_ml = 0, RWKV_PROJ, RWKV_PROJ + 4 * BR_W
    o_if = o_ml + 4 * BR_W
    o_gate = o_if + 2 * N_HEADS
    zpad = lambda n: jnp.zeros((DEPTH, D_MODEL, n), F32)
    w_cat = jnp.concatenate([
        w_in[:, :, o_ret:o_ret + 4 * BR_W], w_in[:, :, o_ml:o_ml + 4 * BR_W], w_in[:, :, o_rw:o_rw + RWKV_PROJ],
        w_in[:, :, o_if:o_if + N_HEADS], zpad(128 - N_HEADS), w_in[:, :, o_if + N_HEADS:o_gate],
        zpad(128 - N_HEADS), w_in[:, :, o_gate:]], axis=2).astype(BF16)
    pad_rows = lambda a, top: jnp.concatenate(
        [jnp.zeros((DEPTH, top, BR_W), F32), a, jnp.zeros((DEPTH, 128 - top - a.shape[1], BR_W), F32)], axis=1)
    w2p = pad_rows(rwkv_w2, 0).astype(BF16)
    a2p = pad_rows(rwkv_a2, W_LORA).astype(BF16)
    g2b = rwkv_g2.astype(BF16)
    pad128 = lambda a: jnp.concatenate([a, jnp.zeros((DEPTH, 128 - N_HEADS), F32)], axis=1)
    wbb, wob, w1b, w2b = (t.astype(BF16) for t in (w_branch, w_out, w_ff1, w_ff2))
    bo = _block_diag_ones(256).astype(BF16)
    bd = _block_diag_ones(128).astype(F32)
    eye = jnp.eye(128, dtype=BF16)

    cos_p, sin_p = _rope_tables(Tp, 0)
    cos_s, sin_s = _rope_tables(Ts, past_len)
    cos_s, sin_s = jnp.tile(cos_s, (SAMPLE_SEQS, 1)), jnp.tile(sin_s, (SAMPLE_SEQS, 1))
    rt_p = _ret_tables(1, Lp)
    rt_s = _ret_tables(SAMPLE_SEQS, Ts)
    tri_p, tri_s = _seq_tri(1, Lp), _seq_tri(SAMPLE_SEQS, Ts)

    tm_in = _pick_tile(R, (1536, 768, 576, 192, 64))
    tm_prep = _pick_tile(R, (256, 64))
    assert Tp % tm_prep == 0
    tm_mix = _pick_tile(R, (512, 192, 64))
    tb_p = min(SCAN_TBLK, Tp)

    zeros_pairs = jnp.zeros((Bp, 4, 128, 128), F32)
    outs = [[] for _ in range(14)]
    for l in range(DEPTH):
        lp = dict(mu=rwkv_mu[l][None], w0=rwkv_w0[l][None], w2=w2p[l], a0=rwkv_a0[l][None], a2=a2p[l], g2=g2b[l],
                  kk=rwkv_kk[l][None], ka=rwkv_ka[l][None], rk=rwkv_rk[l].reshape(1, BR_W),
                  conv_w=mlstm_conv_w[l], conv_b=mlstm_conv_b[l][None], b_i=pad128(mlstm_b_i)[l][None],
                  b_f=pad128(mlstm_b_f)[l][None], mlstm_gn=mlstm_gn_g[l][None])
        P = _inproj(x, norm_mix[l][None], w_cat[l], tm_in, 1024)

        fix = jnp.concatenate([jnp.zeros((tm_prep, RWKV_PROJ), F32),
                               jnp.repeat(state_rwkv_shift[l], Ts, axis=0)], axis=0)
        seqs = _rwkv_prep(P, fix, lp, bo, tm_prep, Rp // tm_prep, Tp, Ts)
        s0_s = state_rwkv_wkv[l].transpose(0, 2, 1, 3).reshape(Bs, HEAD_DIM, BR_W)
        ha_p, wkv_p = _rwkv_scan([a[:Rp].reshape(Bp, Tp, BR_W) for a in seqs],
                                 jnp.zeros((Bp, HEAD_DIM, BR_W), F32), bo, rwkv_ln_g[l][None], rwkv_ln_b[l][None],
                                 tb_p)
        ha_s, wkv_s = _rwkv_scan([a[Rp:].reshape(Bs, Ts, BR_W) for a in seqs], s0_s, bo, rwkv_ln_g[l][None],
                                 rwkv_ln_b[l][None], Ts)
        ha = jnp.concatenate([ha_p.reshape(Rp, BR_W), ha_s.reshape(Rs, BR_W)], axis=0)
        unN = lambda s: s.reshape(-1, HEAD_DIM, N_HEADS, HEAD_DIM).transpose(0, 2, 1, 3)
        pa_rows = P[:, COL_RW:COL_RW + RWKV_PROJ]
        shift_p = pa_rows[:Rp].reshape(Bp, Tp, RWKV_PROJ)[:, -1]
        shift_s = pa_rows[Rp:].reshape(Bs, Ts, RWKV_PROJ)[:, -1]

        gn = ret_gn_g[l][None]
        hr_p, ret_p = _retention(P, 0, Bp, Tp // Lp, 1, Lp, (cos_p, sin_p) + rt_p + (bd,), zeros_pairs, bo, gn)
        hr_s, ret_s = _retention(P, Rp, Bs // SAMPLE_SEQS, 1, SAMPLE_SEQS, Ts, (cos_s, sin_s) + rt_s + (bd,),
                                 _to_pairs(state_ret[l]), bo, gn)
        hr = jnp.concatenate([hr_p, hr_s], axis=0)

        qk_rows = P[:, COL_ML:COL_ML + 2 * BR_W]
        xcs = jnp.concatenate([jnp.zeros((Bs, 1, 2 * BR_W), F32), state_mlstm_conv[l],
                               qk_rows[Rp:].reshape(Bs, Ts, 2 * BR_W)], axis=1)
        hm_p, c_p, n_p, m_p = _mlstm(P, 0, Bp, Tp // Lp, 1, Lp, jnp.zeros((1, 8, 2 * BR_W), F32), lp,
                                     (tri_p, bd, eye), zeros_pairs, jnp.zeros((Bp, 1, BR_W), F32),
                                     jnp.zeros((Bp, 1, 128), F32), bo)
        m0_s = jnp.concatenate([state_mlstm_m[l], jnp.zeros((Bs, 128 - N_HEADS), F32)], axis=1)[:, None]
        hm_s, c_s, n_s, m_s = _mlstm(P, Rp, Bs // SAMPLE_SEQS, 1, SAMPLE_SEQS, Ts, xcs, lp, (tri_s, bd, eye),
                                     _to_pairs(state_mlstm_C[l]), state_mlstm_n[l].reshape(Bs, 1, BR_W), m0_s, bo)
        hm = jnp.concatenate([hm_p, hm_s], axis=0)
        conv_p = qk_rows[:Rp].reshape(Bp, Tp, 2 * BR_W)[:, Tp - (CONV_W - 1):]
        conv_s = xcs[:, 1 + Ts:]

        x = _mix(x, ha, hr, hm, P, wbb[l], b_gate[l][:, None], wob[l], tm_mix)
        x = _ffn(x, norm_ffn[l][None], w1b[l], w2b[l], norm_final[None], tm_in, 512, l == DEPTH - 1)

        for i, t in enumerate((unN(wkv_p), shift_p, _from_pairs(ret_p), _from_pairs(c_p),
                               n_p.reshape(Bp, N_HEADS, HEAD_DIM), m_p[:, 0, :N_HEADS], conv_p,
                               unN(wkv_s), shift_s, _from_pairs(ret_s), _from_pairs(c_s),
                               n_s.reshape(Bs, N_HEADS, HEAD_DIM), m_s[:, 0, :N_HEADS], conv_s)):
            outs[i].append(t)

    y_p = x[:Rp].reshape(Bp, Tp, D_MODEL)
    y_s = x[Rp:].reshape(Bs, Ts, D_MODEL)
    return (y_p, y_s) + tuple(jnp.stack(o, axis=0) for o in outs)
```

```python
import functools
import math

import jax
import jax.numpy as jnp
from jax import lax
from jax.experimental import pallas as pl
from jax.experimental.pallas import tpu as pltpu

F32 = jnp.float32
BF16 = jnp.bfloat16

D_MODEL = 1024
DEPTH = 4
HEAD_DIM = 64
N_HEADS = 8
BR_W = N_HEADS * HEAD_DIM
W_LORA = 64
A_LORA = 64
G_LORA = 128
CONV_W = 4
D_FF = 4 * D_MODEL
ROPE_BASE = 10000.0
RMS_EPS = 1e-6
RWKV_GN_EPS = 64e-5
GN_EPS = 1e-5
RWKV_PROJ = 3 * BR_W + W_LORA + A_LORA + G_LORA

COL_RET = 0
COL_ML = 2048
COL_RW = 4096
COL_IF = COL_RW + RWKV_PROJ
COL_GATE = 6144
N_PROJ = COL_GATE + 3 * D_MODEL

PROMPT_CHUNK = 128
SAMPLE_SEQS = 8
SCAN_BATCH = 8
SCAN_TBLK = 64
VMEM_LIMIT = 56 * 1024 * 1024


def _cparams(sem):
    return pltpu.CompilerParams(dimension_semantics=sem, vmem_limit_bytes=VMEM_LIMIT)


def _dot(a, b):
    return jnp.dot(a, b, preferred_element_type=F32)


def _dot_nt(a, b):
    return lax.dot_general(a, b, (((1,), (1,)), ((), ())), preferred_element_type=F32)


def _dot_tn(a, b):
    return lax.dot_general(a, b, (((0,), (0,)), ((), ())), preferred_element_type=F32)


def _split2(x):
    hi = x.astype(BF16)
    lo = (x - hi.astype(F32)).astype(BF16)
    return hi, lo


def _split3(x):
    x1 = x.astype(BF16)
    r1 = x - x1.astype(F32)
    x2 = r1.astype(BF16)
    x3 = (r1 - x2.astype(F32)).astype(BF16)
    return x1, x2, x3


def _sigmoid(x):
    return 1.0 / (1.0 + jnp.exp(-x))


def _softplus(x):
    return jnp.maximum(x, 0.0) + jnp.log(1.0 + jnp.exp(-jnp.abs(x)))


def _seg_bf16(xb, bo):
    return jnp.concatenate([_dot(xb[:, :256], bo), _dot(xb[:, 256:], bo)], axis=1)


def _segsum(x, bo):
    hi, lo = _split2(x)
    return _seg_bf16(hi, bo) + _seg_bf16(lo, bo)


def _head_norm(h, bo, eps):
    mu = _segsum(h, bo) * (1.0 / HEAD_DIM)
    d = h - mu
    var = _segsum(d * d, bo) * (1.0 / HEAD_DIM)
    return d * lax.rsqrt(var + eps)


def _rms(x, g):
    return x * lax.rsqrt(jnp.mean(x * x, axis=-1, keepdims=True) + RMS_EPS) * g


def _inproj_kernel(x_ref, g_ref, w_ref, o_ref, xn_scr):
    @pl.when(pl.program_id(1) == 0)
    def _():
        xn_scr[...] = _rms(x_ref[...], g_ref[...]).astype(BF16)

    o_ref[...] = _dot(xn_scr[...], w_ref[...])


def _inproj(x, g, w, tm, tn):
    R = x.shape[0]
    return pl.pallas_call(
        _inproj_kernel,
        grid=(R // tm, N_PROJ // tn),
        in_specs=[pl.BlockSpec((tm, D_MODEL), lambda i, j: (i, 0)),
                  pl.BlockSpec((1, D_MODEL), lambda i, j: (0, 0)),
                  pl.BlockSpec((D_MODEL, tn), lambda i, j: (0, j))],
        out_specs=pl.BlockSpec((tm, tn), lambda i, j: (i, j)),
        out_shape=jax.ShapeDtypeStruct((R, N_PROJ), F32),
        scratch_shapes=[pltpu.VMEM((tm, D_MODEL), BF16)],
        compiler_params=_cparams(("parallel", "arbitrary")),
        name="inproj",
    )(x, g, w)


def _rwkv_kernel(*refs, nb, tb, n_p, sample):
    p_refs = refs[:n_p]
    (fix_ref, s0_ref, mu_ref, w0_ref, w2_ref, a0_ref, a2_ref, g2_ref, kkp_ref, kap_ref, rk_ref, bo_ref, mk_ref,
     lng_ref, lnb_ref, ha_ref, sT_ref, S_scr, seq_scr, o_scr, carry_scr) = refs[n_p:]
    j = pl.program_id(1)
    rows = nb * tb

    @pl.when(j == 0)
    def _():
        S_scr[...] = s0_ref[...]
        carry_scr[...] = jnp.zeros_like(carry_scr)

    if sample:
        x = p_refs[0][:, :RWKV_PROJ]
        tok = lax.broadcasted_iota(jnp.int32, (rows, 1), 0) & (tb - 1)
        prev = jnp.where(tok == 0, fix_ref[...], pltpu.roll(x, 1, axis=0))
    else:
        first = lax.broadcasted_iota(jnp.int32, (tb, 1), 0) == 0
        xs = [p_refs[b][:, :RWKV_PROJ] for b in range(nb)]
        prev = jnp.concatenate([jnp.where(first, carry_scr[b:b + 1, :], pltpu.roll(xs[b], 1, axis=0))
                                for b in range(nb)], axis=0)
        for b in range(nb):
            carry_scr[b:b + 1, :] = xs[b][tb - 1:tb, :]
        x = jnp.concatenate(xs, axis=0)
    pm = x + (prev - x) * mu_ref[...]
    r = pm[:, 0:512]
    k = pm[:, 512:1024]
    v = pm[:, 1024:1536]
    wa = pm[:, 1536:1664]
    gd = pm[:, 1664:1792]
    bo = bo_ref[...]
    lw = _dot(jnp.tanh(wa).astype(BF16), w2_ref[...])
    w_log = -_softplus(-(w0_ref[...] + lw)) - 0.5
    a = _sigmoid(a0_ref[...] + _dot(wa.astype(BF16), a2_ref[...]))
    kk = k * kkp_ref[...]
    kk = kk / jnp.maximum(jnp.sqrt(_segsum(kk * kk, bo)), 1e-12)
    kt = k * (1.0 + (a - 1.0) * kap_ref[...])
    seq_scr[0] = r
    seq_scr[1] = jnp.exp(-jnp.exp(w_log))
    seq_scr[2] = kt
    seq_scr[3] = -(kk * a)
    seq_scr[4] = kk
    seq_scr[6] = _dot(_sigmoid(gd).astype(BF16), g2_ref[...])
    seq_scr[7] = _segsum(r * kt * rk_ref[...], bo) * v
    v_hi = v.astype(BF16).astype(F32)
    v_lo = _swap_halves(v - v_hi)
    low_half = (lax.broadcasted_iota(jnp.int32, (1, BR_W), 1) & (HEAD_DIM - 1)) < HEAD_DIM // 2
    seq_scr[5] = jnp.where(low_half, v_hi, v_lo)
    seq_scr[8] = jnp.where(low_half, v_lo, v_hi)

    blk = lambda arr, b: arr[b * HEAD_DIM:(b + 1) * HEAD_DIM]
    half = HEAD_DIM // 2

    def step(t, carry):
        row = lambda q, b: seq_scr[q, pl.ds(b * tb + t, 1), :]
        sk = _seg_bf16(jnp.concatenate([(S_scr[b] * row(4, b)).astype(BF16) for b in range(nb)], axis=0), bo)
        lhs = []
        for b in range(nb):
            lhs.append((row(5, b) * mk_ref[0, :half]).astype(BF16))
            lhs.append((row(8, b) * mk_ref[0, half:]).astype(BF16))
        vc = _seg_bf16(jnp.concatenate(lhs, axis=0), bo)
        p2 = []
        for b in range(nb):
            Sn = S_scr[b] * row(1, b) + blk(sk, b) * row(3, b) + blk(vc, b) * row(2, b)
            S_scr[b] = Sn
            p2.append((Sn * row(0, b)).astype(BF16))
        G = _seg_bf16(jnp.concatenate(p2, axis=0), bo)
        for b in range(nb):
            o_scr[pl.ds(b * tb + t, 1), :] = jnp.sum(blk(G, b) * mk_ref[1], axis=0, keepdims=True)
        return carry

    lax.fori_loop(0, tb, step, 0)

    o = _head_norm(o_scr[...], bo, RWKV_GN_EPS) * lng_ref[...] + lnb_ref[...]
    ha = (o + seq_scr[7]) * seq_scr[6]
    if sample:
        ha_ref[...] = ha
    else:
        for b in range(nb):
            ha_ref[b] = ha[b * tb:(b + 1) * tb]

    @pl.when(j == pl.num_programs(1) - 1)
    def _():
        sT_ref[...] = S_scr[...]


def _rwkv(P, row0, B, T, tb, fix, s0, lp, bo, sample):
    nb = SCAN_BATCH
    cb = COL_RW // 2048
    if sample:
        assert tb == T
        p_specs = [pl.BlockSpec((nb * T, 2048), lambda i, j: (row0 // (nb * T) + i, cb))]
        fix_spec = pl.BlockSpec((nb * T, RWKV_PROJ), lambda i, j: (i, 0))
        ha_spec = pl.BlockSpec((nb * T, BR_W), lambda i, j: (i, 0))
        ha_shape = jax.ShapeDtypeStruct((B * T, BR_W), F32)
    else:
        p_specs = [pl.BlockSpec((tb, 2048), lambda i, j, b=b: (row0 // tb + (i * nb + b) * (T // tb) + j, cb))
                   for b in range(nb)]
        fix_spec = pl.BlockSpec(fix.shape, lambda i, j: (0, 0))
        ha_spec = pl.BlockSpec((nb, tb, BR_W), lambda i, j: (i, j, 0))
        ha_shape = jax.ShapeDtypeStruct((B, T, BR_W), F32)
    row = lambda n: pl.BlockSpec((1, n), lambda i, j: (0, 0))
    full = lambda a, b: pl.BlockSpec((a, b), lambda i, j: (0, 0))
    st = pl.BlockSpec((nb, HEAD_DIM, BR_W), lambda i, j: (i, 0, 0))
    kern = functools.partial(_rwkv_kernel, nb=nb, tb=tb, n_p=len(p_specs), sample=sample)
    return pl.pallas_call(
        kern,
        grid=(B // nb, T // tb),
        in_specs=p_specs + [fix_spec, st, row(RWKV_PROJ), row(BR_W), full(128, BR_W), row(BR_W), full(128, BR_W),
                            full(128, BR_W), row(BR_W), row(BR_W), row(BR_W), full(256, 256),
                            pl.BlockSpec((2, HEAD_DIM, BR_W), lambda i, j: (0, 0, 0)), row(BR_W), row(BR_W)],
        out_specs=[ha_spec, st],
        out_shape=[ha_shape, jax.ShapeDtypeStruct((B, HEAD_DIM, BR_W), F32)],
        scratch_shapes=[pltpu.VMEM((nb, HEAD_DIM, BR_W), F32), pltpu.VMEM((9, nb * tb, BR_W), F32),
                        pltpu.VMEM((nb * tb, BR_W), F32), pltpu.VMEM((nb, RWKV_PROJ), F32)],
        compiler_params=_cparams(("parallel", "arbitrary")),
        name="rwkv",
    )(*([P] * len(p_specs)), fix, s0, lp['mu'], lp['w0'], lp['w2'], lp['a0'], lp['a2'], lp['g2'], lp['kk'],
      lp['ka'], lp['rk'], bo, _diag_masks(), lp['ln_g'], lp['ln_b'])


def _diag_masks():
    u = jnp.arange(HEAD_DIM)[:, None]
    j = jnp.arange(BR_W)[None, :] % HEAD_DIM
    diag = j == u
    return jnp.stack([diag | (j == (u + HEAD_DIM // 2) % HEAD_DIM), diag]).astype(F32)


def _pair_masks():
    lane = lax.broadcasted_iota(jnp.int32, (1, 128), 1)
    m0 = (lane < HEAD_DIM).astype(F32)
    return m0, 1.0 - m0


def _swap_halves(x):
    lane = lax.broadcasted_iota(jnp.int32, x.shape, 1)
    n = x.shape[1]
    return jnp.where((lane & 63) < 32, pltpu.roll(x, n - 32, axis=1), pltpu.roll(x, 32, axis=1))


def _ret_kernel(q_ref, k_ref, v_ref, z_ref, cos_ref, sin_ref, dm_ref, qin_ref, kend_ref, gs_ref, bd_ref, s0_ref,
                bo_ref, gn_ref, h_ref, sT_ref, S_scr, *, nseq, L):
    c = pl.program_id(1)

    @pl.when(c == 0)
    def _():
        S_scr[...] = s0_ref[...]

    cos = cos_ref[...]
    sin = sin_ref[...]
    q = q_ref[...]
    k = k_ref[...]
    q = q * cos + _swap_halves(q) * sin
    k = (k * cos + _swap_halves(k) * sin) * (HEAD_DIM ** -0.5)
    v = v_ref[...]
    qin = qin_ref[...]
    ke = k * kend_ref[...]
    m0, m1 = _pair_masks()
    rows = nseq * L
    rowi = lax.broadcasted_iota(jnp.int32, (rows, 1), 0)
    outs = []
    for p in range(N_HEADS // 2):
        sl = slice(128 * p, 128 * (p + 1))
        qp, kp, vp, kep = q[:, sl], k[:, sl], v[:, sl], ke[:, sl]
        kb = kp.astype(BF16)
        o = jnp.zeros((rows, 128), F32)
        for e, me in enumerate((m0, m1)):
            sc = _dot_nt((qp * me).astype(BF16), kb) * dm_ref[2 * p + e]
            o = o + _dot(sc.astype(BF16), (vp * me).astype(BF16))
        qb = qp.astype(BF16)
        for s in range(nseq):
            S = S_scr[s, p]
            oi = _dot(qb, S.astype(BF16)) * qin[:, sl]
            if nseq == 1:
                o = o + oi
                kes = kep
            else:
                inseq = (rowi >= s * L) & (rowi < (s + 1) * L)
                o = o + jnp.where(inseq, oi, 0.0)
                kes = jnp.where(inseq, kep, 0.0)
            S_scr[s, p] = S * gs_ref[p] + bd_ref[...] * _dot_tn(kes.astype(BF16), vp.astype(BF16))
        outs.append(o)
    o = jnp.concatenate(outs, axis=1)
    o = _head_norm(o, bo_ref[...], GN_EPS) * gn_ref[...]
    z = z_ref[...]
    h_ref[...] = z * _sigmoid(z) * o

    @pl.when(c == pl.num_programs(1) - 1)
    def _():
        sT_ref[...] = S_scr[...]


def _retention(P, row0, n_groups, n_chunks, nseq, L, tabs, s0, bo, gn):
    rows = nseq * L
    rb0 = row0 // rows
    cos, sin, dm, qin, kend, gs, bd = tabs
    col = lambda cb: pl.BlockSpec((rows, BR_W), lambda i, c: (rb0 + i * n_chunks + c, cb))
    tab = pl.BlockSpec((rows, BR_W), lambda i, c: (c, 0))
    cst = lambda shp: pl.BlockSpec(shp, lambda i, c: (0,) * len(shp))
    st = pl.BlockSpec((nseq, 4, 128, 128), lambda i, c: (i, 0, 0, 0))
    kern = functools.partial(_ret_kernel, nseq=nseq, L=L)
    return pl.pallas_call(
        kern,
        grid=(n_groups, n_chunks),
        in_specs=[col(0), col(1), col(2), col(3), tab, tab, cst((N_HEADS, rows, rows)), cst((rows, BR_W)),
                  cst((rows, BR_W)), cst((4, 128, 128)), cst((128, 128)), st, cst((256, 256)), cst((1, BR_W))],
        out_specs=[pl.BlockSpec((rows, BR_W), lambda i, c: (i * n_chunks + c, 0)), st],
        out_shape=[jax.ShapeDtypeStruct((n_groups * n_chunks * rows, BR_W), F32),
                   jax.ShapeDtypeStruct((n_groups * nseq, 4, 128, 128), F32)],
        scratch_shapes=[pltpu.VMEM((nseq, 4, 128, 128), F32)],
        compiler_params=_cparams(("parallel", "arbitrary")),
        name="retention",
    )(P, P, P, P, cos, sin, dm, qin, kend, gs, bd, s0, bo, gn)


def _mlstm_kernel(qk_ref, v_ref, o_ref, if_ref, xcs_ref, cw_ref, cb_ref, bi_ref, bf_ref, tri_ref, bd_ref, eye_ref,
                  c0_ref, n0_ref, m0_ref, bo_ref, gn_ref, h_ref, cT_ref, nT_ref, mT_ref,
                  C_scr, n_scr, m_scr, xc_scr, *, nseq, L):
    c = pl.program_id(1)
    rows = nseq * L

    @pl.when(c == 0)
    def _():
        C_scr[...] = c0_ref[...]
        n_scr[...] = n0_ref[...]
        m_scr[...] = m0_ref[...]
        if nseq == 1:
            xc_scr[0:8, :] = jnp.zeros((8, 2 * BR_W), F32)

    cw = cw_ref[...]
    if nseq == 1:
        xc_scr[8:8 + L, :] = qk_ref[...]
        conv = cb_ref[...]
        for w in range(CONV_W):
            conv = conv + xc_scr[5 + w:5 + w + L, :] * cw[w:w + 1, :]
        xc_scr[0:8, :] = xc_scr[L:L + 8, :]
    else:
        parts = []
        for s in range(nseq):
            cs = cb_ref[...]
            for w in range(CONV_W):
                cs = cs + xcs_ref[s, 1 + w:1 + w + L, :] * cw[w:w + 1, :]
            parts.append(cs)
        conv = jnp.concatenate(parts, axis=0)
    qk = conv * _sigmoid(conv)
    q = qk[:, :BR_W]
    k = qk[:, BR_W:] * (HEAD_DIM ** -0.5)
    v = v_ref[...]

    rowi = lax.broadcasted_iota(jnp.int32, (rows, 1), 0)
    tok = rowi % L
    ig = if_ref[:, 0:128] + bi_ref[...]
    fp = if_ref[:, 128:256] + bf_ref[...]
    lf = jnp.minimum(fp, 0.0) - jnp.log(1.0 + jnp.exp(-jnp.abs(fp)))
    tri = tri_ref[...]
    l1, l2, l3 = _split3(lf)
    bcum = _dot(tri, l1) + _dot(tri, l2) + _dot(tri, l3)
    u = ig - bcum
    cm = u
    sh = 1
    while sh < L:
        cm = jnp.maximum(cm, jnp.where(tok >= sh, pltpu.roll(cm, sh, axis=0), -jnp.inf))
        sh *= 2
    if nseq == 1:
        m0r = m_scr[0]
    else:
        m0r = jnp.concatenate([jnp.broadcast_to(m_scr[s], (L, 128)) for s in range(nseq)], axis=0)
    mt = bcum + jnp.maximum(m0r, cm)
    inter = jnp.exp(bcum + m0r - mt)
    bmm = bcum - mt
    emt = jnp.exp(-mt)
    wend_parts = []
    for s in range(nseq):
        last = slice(s * L + L - 1, s * L + L)
        m_end = mt[last, :]
        wend_parts.append(jnp.exp(bcum[last, :] - m_end + u[s * L:(s + 1) * L, :]))
    wend = wend_parts[0] if nseq == 1 else jnp.concatenate(wend_parts, axis=0)
    eye = eye_ref[...]
    u1, u2, u3 = _split3(u)
    ut = _dot_nt(eye, u1) + _dot_nt(eye, u2) + _dot_nt(eye, u3)
    trib = tri > 0.5

    m0m, m1m = _pair_masks()
    lane128 = lax.broadcasted_iota(jnp.int32, (1, 128), 1)
    first = lane128 < HEAD_DIM
    bo = bo_ref[...]
    qn_all = None
    outs = []
    wk_all = []
    for p in range(N_HEADS // 2):
        sl = slice(128 * p, 128 * (p + 1))
        qp, kp, vp = q[:, sl], k[:, sl], v[:, sl]
        kb = kp.astype(BF16)
        num = jnp.zeros((rows, 128), F32)
        dsum = []
        for e, me in enumerate((m0m, m1m)):
            h = 2 * p + e
            logd = bmm[:, h:h + 1] + ut[h:h + 1, :]
            dmat = jnp.where(trib, jnp.exp(jnp.minimum(logd, 0.0)), 0.0)
            sc = _dot_nt((qp * me).astype(BF16), kb) * dmat
            num = num + _dot(sc.astype(BF16), (vp * me).astype(BF16))
            dsum.append(jnp.sum(sc, axis=1, keepdims=True))
        pick = lambda arr: jnp.where(first, arr[:, 2 * p:2 * p + 1], arr[:, 2 * p + 1:2 * p + 2])
        inter_p = pick(inter)
        den = jnp.where(first, dsum[0], dsum[1])
        wend_p = pick(wend)
        qb = qp.astype(BF16)
        kw = kp * wend_p
        wk_all.append(kw)
        qn = jnp.zeros((rows, 128), F32)
        for s in range(nseq):
            C = C_scr[s, p]
            nrow = n_scr[s][:, sl]
            qc = _dot(qb, C.astype(BF16))
            qns = qp * nrow
            if nseq == 1:
                num = num + inter_p * qc
                qn = qns
                kws = kw
                send = inter_p[L - 1:L, :]
            else:
                inseq = (rowi >= s * L) & (rowi < (s + 1) * L)
                num = num + jnp.where(inseq, inter_p * qc, 0.0)
                qn = qn + jnp.where(inseq, qns, 0.0)
                kws = jnp.where(inseq, kw, 0.0)
                send = inter_p[s * L + L - 1:s * L + L, :]
            send_col = jnp.where(lax.broadcasted_iota(jnp.int32, (128, 1), 0) < HEAD_DIM,
                                 send[:, 0:1], send[:, 64:65])
            C_scr[s, p] = C * send_col + bd_ref[...] * _dot_tn(kws.astype(BF16), vp.astype(BF16))
        den = den + inter_p * _segsum_pair(qn, bo)
        outs.append(num / jnp.maximum(jnp.abs(den), pick(emt)))
    hh = jnp.concatenate(outs, axis=1)
    kw_full = jnp.concatenate(wk_all, axis=1)
    for s in range(nseq):
        last = slice(s * L + L - 1, s * L + L)
        send_row = _expand_heads(inter[last, :])
        n_scr[s] = send_row * n_scr[s] + jnp.sum(kw_full[s * L:(s + 1) * L, :], axis=0, keepdims=True)
        m_scr[s] = mt[last, :]
    op = o_ref[...]
    hh = _sigmoid(op) * hh
    h_ref[...] = _head_norm(hh, bo, GN_EPS) * gn_ref[...]

    @pl.when(c == pl.num_programs(1) - 1)
    def _():
        cT_ref[...] = C_scr[...]
        nT_ref[...] = n_scr[...]
        mT_ref[...] = m_scr[...]


def _segsum_pair(x, bo):
    hi, lo = _split2(x)
    b2 = bo[:128, :128]
    return _dot(hi, b2) + _dot(lo, b2)


def _expand_heads(row):
    lane = lax.broadcasted_iota(jnp.int32, (1, BR_W), 1)
    out = jnp.zeros((1, BR_W), F32)
    for h in range(N_HEADS):
        out = jnp.where((lane >> 6) == h, row[:, h:h + 1], out)
    return out


def _mlstm(P, row0, n_groups, n_chunks, nseq, L, xcs, lp, tabs, c0, n0, m0, bo):
    rows = nseq * L
    rb0 = row0 // rows
    tri, bd, eye = tabs
    rowblk = lambda width, cb: pl.BlockSpec((rows, width), lambda i, c: (rb0 + i * n_chunks + c, cb))
    cst = lambda shp: pl.BlockSpec(shp, lambda i, c: (0,) * len(shp))
    stC = pl.BlockSpec((nseq, 4, 128, 128), lambda i, c: (i, 0, 0, 0))
    stn = pl.BlockSpec((nseq, 1, BR_W), lambda i, c: (i, 0, 0))
    stm = pl.BlockSpec((nseq, 1, 128), lambda i, c: (i, 0, 0))
    if nseq == 1:
        xcs_spec = cst((1, 8, 2 * BR_W))
    else:
        xcs_spec = pl.BlockSpec((nseq, 8, 2 * BR_W), lambda i, c: (i, 0, 0))
    kern = functools.partial(_mlstm_kernel, nseq=nseq, L=L)
    nB = n_groups * nseq
    return pl.pallas_call(
        kern,
        grid=(n_groups, n_chunks),
        in_specs=[rowblk(2 * BR_W, COL_ML // (2 * BR_W)), rowblk(BR_W, (COL_ML + 1024) // BR_W),
                  rowblk(BR_W, (COL_ML + 1536) // BR_W), rowblk(256, COL_IF // 256), xcs_spec,
                  cst((CONV_W, 2 * BR_W)), cst((1, 2 * BR_W)), cst((1, 128)), cst((1, 128)),
                  cst((rows, rows)), cst((128, 128)), cst((128, 128)), stC, stn, stm, cst((256, 256)),
                  cst((1, BR_W))],
        out_specs=[pl.BlockSpec((rows, BR_W), lambda i, c: (i * n_chunks + c, 0)), stC, stn, stm],
        out_shape=[jax.ShapeDtypeStruct((n_groups * n_chunks * rows, BR_W), F32),
                   jax.ShapeDtypeStruct((nB, 4, 128, 128), F32),
                   jax.ShapeDtypeStruct((nB, 1, BR_W), F32),
                   jax.ShapeDtypeStruct((nB, 1, 128), F32)],
        scratch_shapes=[pltpu.VMEM((nseq, 4, 128, 128), F32), pltpu.VMEM((nseq, 1, BR_W), F32),
                        pltpu.VMEM((nseq, 1, 128), F32), pltpu.VMEM((L + 8, 2 * BR_W), F32)],
        compiler_params=_cparams(("parallel", "arbitrary")),
        name="mlstm",
    )(P, P, P, P, xcs, lp['conv_w'], lp['conv_b'], lp['b_i'], lp['b_f'], tri, bd, eye, c0, n0, m0, bo,
      lp['mlstm_gn'])


def _mix_kernel(x_ref, hap_ref, has_ref, hrp_ref, hrs_ref, hmp_ref, hms_ref, g0_ref, g1_ref, g2_ref, wb_ref,
                bg_ref, wo_ref, o_ref, *, n_p):
    def body(h_refs):
        mixed = None
        for n, (h_ref, pg_ref) in enumerate(zip(h_refs, (g0_ref, g1_ref, g2_ref))):
            ub = _dot(h_ref[...].astype(BF16), wb_ref[n])
            t = _sigmoid(pg_ref[...] + bg_ref[n]) * ub
            mixed = t if mixed is None else mixed + t
        o_ref[...] = x_ref[...] + _dot(mixed.astype(BF16), wo_ref[...])

    @pl.when(pl.program_id(0) < n_p)
    def _():
        body((hap_ref, hrp_ref, hmp_ref))

    @pl.when(pl.program_id(0) >= n_p)
    def _():
        body((has_ref, hrs_ref, hms_ref))


def _mix(x, hp, hs, P, wb, bg, wo, tm):
    R = x.shape[0]
    n_p = hp[0].shape[0] // tm
    xs = pl.BlockSpec((tm, D_MODEL), lambda i: (i, 0))
    hps = pl.BlockSpec((tm, BR_W), lambda i: (jnp.minimum(i, n_p - 1), 0))
    hss = pl.BlockSpec((tm, BR_W), lambda i: (jnp.maximum(i - n_p, 0), 0))
    gs = lambda n: pl.BlockSpec((tm, D_MODEL), lambda i: (i, COL_GATE // D_MODEL + n))
    return pl.pallas_call(
        functools.partial(_mix_kernel, n_p=n_p),
        grid=(R // tm,),
        in_specs=[xs, hps, hss, hps, hss, hps, hss, gs(0), gs(1), gs(2),
                  pl.BlockSpec((3, BR_W, D_MODEL), lambda i: (0, 0, 0)),
                  pl.BlockSpec((3, 1, D_MODEL), lambda i: (0, 0, 0)),
                  pl.BlockSpec((D_MODEL, D_MODEL), lambda i: (0, 0))],
        out_specs=xs,
        out_shape=jax.ShapeDtypeStruct((R, D_MODEL), F32),
        compiler_params=_cparams(("parallel",)),
        name="mix_out",
    )(x, hp[0], hs[0], hp[1], hs[1], hp[2], hs[2], P, P, P, wb, bg, wo)


def _ffn_kernel(x_ref, g_ref, w1_ref, w2_ref, gf_ref, o_ref, xn_scr, acc_scr, *, final):
    j = pl.program_id(1)

    @pl.when(j == 0)
    def _():
        xn_scr[...] = _rms(x_ref[...], g_ref[...]).astype(BF16)
        acc_scr[...] = jnp.zeros_like(acc_scr)

    h = jnp.maximum(_dot(xn_scr[...], w1_ref[...]), 0.0)
    acc_scr[...] += _dot((h * h).astype(BF16), w2_ref[...])

    @pl.when(j == pl.num_programs(1) - 1)
    def _():
        y = x_ref[...] + acc_scr[...]
        if final:
            y = _rms(y, gf_ref[...])
        o_ref[...] = y


def _ffn(x, g, w1, w2, gf, tm, tf, final):
    R = x.shape[0]
    kern = functools.partial(_ffn_kernel, final=final)
    return pl.pallas_call(
        kern,
        grid=(R // tm, D_FF // tf),
        in_specs=[pl.BlockSpec((tm, D_MODEL), lambda i, j: (i, 0)),
                  pl.BlockSpec((1, D_MODEL), lambda i, j: (0, 0)),
                  pl.BlockSpec((D_MODEL, tf), lambda i, j: (0, j)),
                  pl.BlockSpec((tf, D_MODEL), lambda i, j: (j, 0)),
                  pl.BlockSpec((1, D_MODEL), lambda i, j: (0, 0))],
        out_specs=pl.BlockSpec((tm, D_MODEL), lambda i, j: (i, 0)),
        out_shape=jax.ShapeDtypeStruct((R, D_MODEL), F32),
        scratch_shapes=[pltpu.VMEM((tm, D_MODEL), BF16), pltpu.VMEM((tm, D_MODEL), F32)],
        compiler_params=_cparams(("parallel", "arbitrary")),
        name="ffn",
    )(x, g, w1, w2, gf)


def _rope_tables(T, pos0):
    half = HEAD_DIM // 2
    inv = jnp.power(ROPE_BASE, -jnp.arange(half, dtype=F32) / half)
    ang = (jnp.arange(T, dtype=F32) + pos0)[:, None] * inv[None, :]
    cos, sin = jnp.cos(ang), jnp.sin(ang)
    cos_h = jnp.concatenate([cos, cos], axis=1)
    sin_h = jnp.concatenate([-sin, sin], axis=1)
    return jnp.tile(cos_h, (1, N_HEADS)), jnp.tile(sin_h, (1, N_HEADS))


def _ret_tables(nseq, L):
    log_g = jnp.log(1.0 - jnp.exp2(-5.0 - jnp.arange(N_HEADS, dtype=F32)))
    idx = jnp.arange(L, dtype=F32)
    diff = idx[:, None] - idx[None, :]
    dmask = jnp.where(diff[None] >= 0, jnp.exp(jnp.maximum(diff, 0.0)[None] * log_g[:, None, None]), 0.0)
    q_in = jnp.exp((idx + 1.0)[:, None] * log_g[None, :])
    k_end = jnp.exp((L - 1.0 - idx)[:, None] * log_g[None, :])
    g_chunk = jnp.exp(L * log_g)
    eye = jnp.eye(nseq, dtype=F32)
    dm = jnp.einsum('st,hij->hsitj', eye, dmask).reshape(N_HEADS, nseq * L, nseq * L)
    qin = jnp.tile(jnp.repeat(q_in, HEAD_DIM, axis=1), (nseq, 1))
    kend = jnp.tile(jnp.repeat(k_end, HEAD_DIM, axis=1), (nseq, 1))
    gs = jnp.broadcast_to(jnp.repeat(g_chunk, HEAD_DIM).reshape(4, 128, 1), (4, 128, 128))
    return dm, qin, kend, gs


def _block_diag_ones(n):
    i = jnp.arange(n) // HEAD_DIM
    return (i[:, None] == i[None, :])


def _seq_tri(nseq, L):
    r = jnp.arange(nseq * L)
    return ((r[:, None] // L == r[None, :] // L) & (r[:, None] >= r[None, :])).astype(BF16)


def _to_pairs(S):
    B = S.shape[0]
    S = S.reshape(B, 4, 2, HEAD_DIM, HEAD_DIM)
    eye = jnp.eye(2, dtype=S.dtype)
    return jnp.einsum('bpeij,ef->bpeifj', S, eye).reshape(B, 4, 128, 128)


def _from_pairs(S):
    B = S.shape[0]
    S = S.reshape(B, 4, 2, HEAD_DIM, 2, HEAD_DIM)
    return jnp.stack([S[:, :, 0, :, 0, :], S[:, :, 1, :, 1, :]], axis=2).reshape(B, N_HEADS, HEAD_DIM, HEAD_DIM)


def _pick_tile(n, cands):
    for c in cands:
        if n % c == 0:
            return c
    raise ValueError(f"no tile for {n}")


def kernel(x_prompt, x_sample, state_rwkv_wkv, state_rwkv_shift, state_ret, state_mlstm_C, state_mlstm_n,
           state_mlstm_m, state_mlstm_conv, norm_mix, w_in, rwkv_mu, rwkv_w0, rwkv_w2, rwkv_a0, rwkv_a2,
           rwkv_g2, rwkv_kk, rwkv_ka, rwkv_rk, rwkv_ln_g, rwkv_ln_b, ret_gn_g, mlstm_conv_w, mlstm_conv_b,
           mlstm_b_i, mlstm_b_f, mlstm_gn_g, w_branch, b_gate, w_out, norm_ffn, w_ff1, w_ff2, norm_final):
    Bp, Tp, _ = x_prompt.shape
    Bs, Ts, _ = x_sample.shape
    Rp, Rs = Bp * Tp, Bs * Ts
    R = Rp + Rs
    past_len = 16384
    Lp = min(PROMPT_CHUNK, Tp)
    assert Tp % Lp == 0 and Bs % SAMPLE_SEQS == 0 and Rp % (SAMPLE_SEQS * Ts) == 0
    assert Tp & (Tp - 1) == 0 and Ts & (Ts - 1) == 0 and Tp >= CONV_W - 1 and Bp % SCAN_BATCH == 0

    x = jnp.concatenate([x_prompt.reshape(Rp, D_MODEL), x_sample.reshape(Rs, D_MODEL)], axis=0)

    o_rw, o_ret, o_ml = 0, RWKV_PROJ, RWKV_PROJ + 4 * BR_W
    o_if = o_ml + 4 * BR_W
    o_gate = o_if + 2 * N_HEADS
    zpad = lambda n: jnp.zeros((DEPTH, D_MODEL, n), F32)
    w_cat = jnp.concatenate([
        w_in[:, :, o_ret:o_ret + 4 * BR_W], w_in[:, :, o_ml:o_ml + 4 * BR_W], w_in[:, :, o_rw:o_rw + RWKV_PROJ],
        w_in[:, :, o_if:o_if + N_HEADS], zpad(128 - N_HEADS), w_in[:, :, o_if + N_HEADS:o_gate],
        zpad(128 - N_HEADS), w_in[:, :, o_gate:]], axis=2).astype(BF16)
    pad_rows = lambda a, top: jnp.concatenate(
        [jnp.zeros((DEPTH, top, BR_W), F32), a, jnp.zeros((DEPTH, 128 - top - a.shape[1], BR_W), F32)], axis=1)
    w2p = pad_rows(rwkv_w2, 0).astype(BF16)
    a2p = pad_rows(rwkv_a2, W_LORA).astype(BF16)
    g2b = rwkv_g2.astype(BF16)
    pad128 = lambda a: jnp.concatenate([a, jnp.zeros((DEPTH, 128 - N_HEADS), F32)], axis=1)
    wbb, wob, w1b, w2b = (t.astype(BF16) for t in (w_branch, w_out, w_ff1, w_ff2))
    bo = _block_diag_ones(256).astype(BF16)
    bd = _block_diag_ones(128).astype(F32)
    eye = jnp.eye(128, dtype=BF16)

    cos_p, sin_p = _rope_tables(Tp, 0)
    cos_s, sin_s = _rope_tables(Ts, past_len)
    cos_s, sin_s = jnp.tile(cos_s, (SAMPLE_SEQS, 1)), jnp.tile(sin_s, (SAMPLE_SEQS, 1))
    rt_p = _ret_tables(1, Lp)
    rt_s = _ret_tables(SAMPLE_SEQS, Ts)
    tri_p, tri_s = _seq_tri(1, Lp), _seq_tri(SAMPLE_SEQS, Ts)

    tm_in = _pick_tile(R, (1536, 768, 576, 192, 64))
    tm_mix = _pick_tile(math.gcd(Rp, Rs), (512, 64))
    tb_p = min(SCAN_TBLK, Tp)

    zeros_pairs = jnp.zeros((Bp, 4, 128, 128), F32)
    outs = [[] for _ in range(14)]
    for l in range(DEPTH):
        lp = dict(mu=rwkv_mu[l][None], w0=rwkv_w0[l][None], w2=w2p[l], a0=rwkv_a0[l][None], a2=a2p[l], g2=g2b[l],
                  kk=rwkv_kk[l][None], ka=rwkv_ka[l][None], rk=rwkv_rk[l].reshape(1, BR_W),
                  ln_g=rwkv_ln_g[l][None], ln_b=rwkv_ln_b[l][None],
                  conv_w=mlstm_conv_w[l], conv_b=mlstm_conv_b[l][None], b_i=pad128(mlstm_b_i)[l][None],
                  b_f=pad128(mlstm_b_f)[l][None], mlstm_gn=mlstm_gn_g[l][None])
        P = _inproj(x, norm_mix[l][None], w_cat[l], tm_in, 1024)

        fix_s = jnp.repeat(state_rwkv_shift[l], Ts, axis=0)
        s0_s = state_rwkv_wkv[l].transpose(0, 2, 1, 3).reshape(Bs, HEAD_DIM, BR_W)
        ha_p, wkv_p = _rwkv(P, 0, Bp, Tp, tb_p, jnp.zeros((8, 128), F32), jnp.zeros((Bp, HEAD_DIM, BR_W), F32),
                            lp, bo, False)
        ha_s, wkv_s = _rwkv(P, Rp, Bs, Ts, Ts, fix_s, s0_s, lp, bo, True)
        ha_p = ha_p.reshape(Rp, BR_W)
        unN = lambda s: s.reshape(-1, HEAD_DIM, N_HEADS, HEAD_DIM).transpose(0, 2, 1, 3)
        last_rows = lambda c0, c1, back: (lax.slice(P, (Tp - 1 - back, c0), (Rp, c1), (Tp, 1)),
                                          lax.slice(P, (Rp + Ts - 1 - back, c0), (R, c1), (Ts, 1)))
        shift_p, shift_s = last_rows(COL_RW, COL_RW + RWKV_PROJ, 0)

        gn = ret_gn_g[l][None]
        hr_p, ret_p = _retention(P, 0, Bp, Tp // Lp, 1, Lp, (cos_p, sin_p) + rt_p + (bd,), zeros_pairs, bo, gn)
        hr_s, ret_s = _retention(P, Rp, Bs // SAMPLE_SEQS, 1, SAMPLE_SEQS, Ts, (cos_s, sin_s) + rt_s + (bd,),
                                 _to_pairs(state_ret[l]), bo, gn)

        qk_s = lax.slice(P, (Rp, COL_ML), (R, COL_ML + 2 * BR_W)).reshape(Bs, Ts, 2 * BR_W)
        xcs = jnp.concatenate([jnp.zeros((Bs, 1, 2 * BR_W), F32), state_mlstm_conv[l], qk_s], axis=1)
        hm_p, c_p, n_p, m_p = _mlstm(P, 0, Bp, Tp // Lp, 1, Lp, jnp.zeros((1, 8, 2 * BR_W), F32), lp,
                                     (tri_p, bd, eye), zeros_pairs, jnp.zeros((Bp, 1, BR_W), F32),
                                     jnp.zeros((Bp, 1, 128), F32), bo)
        m0_s = jnp.concatenate([state_mlstm_m[l], jnp.zeros((Bs, 128 - N_HEADS), F32)], axis=1)[:, None]
        hm_s, c_s, n_s, m_s = _mlstm(P, Rp, Bs // SAMPLE_SEQS, 1, SAMPLE_SEQS, Ts, xcs, lp, (tri_s, bd, eye),
                                     _to_pairs(state_mlstm_C[l]), state_mlstm_n[l].reshape(Bs, 1, BR_W), m0_s, bo)
        conv_p = jnp.stack([last_rows(COL_ML, COL_ML + 2 * BR_W, back)[0] for back in (2, 1, 0)], axis=1)
        conv_s = xcs[:, 1 + Ts:]

        x = _mix(x, (ha_p, hr_p, hm_p), (ha_s, hr_s, hm_s), P, wbb[l], b_gate[l][:, None], wob[l], tm_mix)
        x = _ffn(x, norm_ffn[l][None], w1b[l], w2b[l], norm_final[None], tm_in, 512, l == DEPTH - 1)

        for i, t in enumerate((unN(wkv_p), shift_p, _from_pairs(ret_p), _from_pairs(c_p),
                               n_p.reshape(Bp, N_HEADS, HEAD_DIM), m_p[:, 0, :N_HEADS], conv_p,
                               unN(wkv_s), shift_s, _from_pairs(ret_s), _from_pairs(c_s),
                               n_s.reshape(Bs, N_HEADS, HEAD_DIM), m_s[:, 0, :N_HEADS], conv_s)):
            outs[i].append(t)

    y_p = x[:Rp].reshape(Bp, Tp, D_MODEL)
    y_s = x[Rp:].reshape(Bs, Ts, D_MODEL)
    return (y_p, y_s) + tuple(jnp.stack(o, axis=0) for o in outs)
```

```python
import functools
import math

import jax
import jax.numpy as jnp
from jax import lax
from jax.experimental import pallas as pl
from jax.experimental.pallas import tpu as pltpu

F32 = jnp.float32
BF16 = jnp.bfloat16

D_MODEL = 1024
DEPTH = 4
HEAD_DIM = 64
N_HEADS = 8
BR_W = N_HEADS * HEAD_DIM
W_LORA = 64
A_LORA = 64
G_LORA = 128
CONV_W = 4
D_FF = 4 * D_MODEL
ROPE_BASE = 10000.0
RMS_EPS = 1e-6
RWKV_GN_EPS = 64e-5
GN_EPS = 1e-5
RWKV_PROJ = 3 * BR_W + W_LORA + A_LORA + G_LORA

COL_RET = 0
COL_ML = 2048
COL_RW = 4096
COL_IF = COL_RW + RWKV_PROJ
COL_GATE = 6144
N_PROJ = COL_GATE + 3 * D_MODEL

PROMPT_CHUNK = 128
SAMPLE_SEQS = 8
SCAN_BATCH = 8
SCAN_TBLK = 64
SCAN_GROUP = 8
VMEM_LIMIT = 56 * 1024 * 1024


def _cparams(sem):
    return pltpu.CompilerParams(dimension_semantics=sem, vmem_limit_bytes=VMEM_LIMIT)


def _dot(a, b):
    return jnp.dot(a, b, preferred_element_type=F32)


def _dot_nt(a, b):
    return lax.dot_general(a, b, (((1,), (1,)), ((), ())), preferred_element_type=F32)


def _dot_tn(a, b):
    return lax.dot_general(a, b, (((0,), (0,)), ((), ())), preferred_element_type=F32)


def _split2(x):
    hi = x.astype(BF16)
    lo = (x - hi.astype(F32)).astype(BF16)
    return hi, lo


def _split3(x):
    x1 = x.astype(BF16)
    r1 = x - x1.astype(F32)
    x2 = r1.astype(BF16)
    x3 = (r1 - x2.astype(F32)).astype(BF16)
    return x1, x2, x3


def _sigmoid(x):
    return 1.0 / (1.0 + jnp.exp(-x))


def _softplus(x):
    return jnp.maximum(x, 0.0) + jnp.log(1.0 + jnp.exp(-jnp.abs(x)))


def _seg_bf16(xb, bo):
    return jnp.concatenate([_dot(xb[:, :256], bo), _dot(xb[:, 256:], bo)], axis=1)


def _segsum(x, bo):
    hi, lo = _split2(x)
    return _seg_bf16(hi, bo) + _seg_bf16(lo, bo)


def _head_norm(h, bo, eps):
    mu = _segsum(h, bo) * (1.0 / HEAD_DIM)
    d = h - mu
    var = _segsum(d * d, bo) * (1.0 / HEAD_DIM)
    return d * lax.rsqrt(var + eps)


def _rms(x, g):
    return x * lax.rsqrt(jnp.mean(x * x, axis=-1, keepdims=True) + RMS_EPS) * g


def _inproj_kernel(x_ref, g_ref, w_ref, o_ref, xn_scr):
    @pl.when(pl.program_id(1) == 0)
    def _():
        xn_scr[...] = _rms(x_ref[...], g_ref[...]).astype(BF16)

    o_ref[...] = _dot(xn_scr[...], w_ref[...])


def _inproj(x, g, w, tm, tn):
    R = x.shape[0]
    return pl.pallas_call(
        _inproj_kernel,
        grid=(R // tm, N_PROJ // tn),
        in_specs=[pl.BlockSpec((tm, D_MODEL), lambda i, j: (i, 0)),
                  pl.BlockSpec((1, D_MODEL), lambda i, j: (0, 0)),
                  pl.BlockSpec((D_MODEL, tn), lambda i, j: (0, j))],
        out_specs=pl.BlockSpec((tm, tn), lambda i, j: (i, j)),
        out_shape=jax.ShapeDtypeStruct((R, N_PROJ), F32),
        scratch_shapes=[pltpu.VMEM((tm, D_MODEL), BF16)],
        compiler_params=_cparams(("parallel", "arbitrary")),
        name="inproj",
    )(x, g, w)


def _rwkv_kernel(*refs, nb, tb, n_p, sample):
    p_refs = refs[:n_p]
    (fix_ref, s0_ref, mu_ref, w0_ref, w2_ref, a0_ref, a2_ref, g2_ref, kkp_ref, kap_ref, rk_ref, bo_ref, mk_ref,
     lng_ref, lnb_ref, ha_ref, sT_ref, shift_ref, S_scr, seq_scr, o_scr, carry_scr, o8_scr) = refs[n_p:]
    j = pl.program_id(1)
    rows = nb * tb

    @pl.when(j == 0)
    def _():
        for b in range(nb):
            for h in range(N_HEADS):
                S_scr[b, :, h * HEAD_DIM:(h + 1) * HEAD_DIM] = s0_ref[b, h]
        carry_scr[...] = jnp.zeros_like(carry_scr)

    if sample:
        x = p_refs[0][:, :RWKV_PROJ]
        tok = lax.broadcasted_iota(jnp.int32, (rows, 1), 0) & (tb - 1)
        prev = jnp.where(tok == 0, fix_ref[...], pltpu.roll(x, 1, axis=0))
    else:
        first = lax.broadcasted_iota(jnp.int32, (tb, 1), 0) == 0
        xs = [p_refs[b][:, :RWKV_PROJ] for b in range(nb)]
        prev = jnp.concatenate([jnp.where(first, carry_scr[b:b + 1, :], pltpu.roll(xs[b], 1, axis=0))
                                for b in range(nb)], axis=0)
        for b in range(nb):
            carry_scr[b:b + 1, :] = xs[b][tb - 1:tb, :]
        x = jnp.concatenate(xs, axis=0)
    pm = x + (prev - x) * mu_ref[...]
    r = pm[:, 0:512]
    k = pm[:, 512:1024]
    v = pm[:, 1024:1536]
    wa = pm[:, 1536:1664]
    gd = pm[:, 1664:1792]
    bo = bo_ref[...]
    lw = _dot(jnp.tanh(wa).astype(BF16), w2_ref[...])
    w_log = -_softplus(-(w0_ref[...] + lw)) - 0.5
    a = _sigmoid(a0_ref[...] + _dot(wa.astype(BF16), a2_ref[...]))
    kk = k * kkp_ref[...]
    kk = kk / jnp.maximum(jnp.sqrt(_segsum(kk * kk, bo)), 1e-12)
    kt = k * (1.0 + (a - 1.0) * kap_ref[...])
    seq_scr[0] = r
    seq_scr[1] = jnp.exp(-jnp.exp(w_log))
    seq_scr[2] = kt
    seq_scr[3] = -(kk * a)
    seq_scr[4] = kk
    seq_scr[6] = _dot(_sigmoid(gd).astype(BF16), g2_ref[...])
    seq_scr[7] = _segsum(r * kt * rk_ref[...], bo) * v
    v_hi = v.astype(BF16).astype(F32)
    v_lo = _swap_halves(v - v_hi)
    low_half = (lax.broadcasted_iota(jnp.int32, (1, BR_W), 1) & (HEAD_DIM - 1)) < HEAD_DIM // 2
    seq_scr[5] = jnp.where(low_half, v_hi, v_lo)
    seq_scr[8] = jnp.where(low_half, v_lo, v_hi)

    blk = lambda arr, b: arr[b * HEAD_DIM:(b + 1) * HEAD_DIM]
    half = HEAD_DIM // 2
    head_mask = (lax.broadcasted_iota(jnp.int32, (N_HEADS, BR_W), 1) // HEAD_DIM
                 == lax.broadcasted_iota(jnp.int32, (N_HEADS, BR_W), 0)).astype(F32)

    def emit_out(t):
        for b in range(nb):
            r8 = (seq_scr[0, pl.ds(b * tb + t, 1), :] * head_mask).astype(BF16)
            o8 = _dot_nt(r8, S_scr[b].astype(BF16))
            o8_scr[pl.ds(pl.multiple_of((b * tb + t) * N_HEADS, N_HEADS), N_HEADS), :] = o8

    def step(t, carry):
        emit_out(jnp.maximum(t - 1, 0))
        row = lambda q, b: seq_scr[q, pl.ds(b * tb + t, 1), :]
        for g0 in range(0, nb, SCAN_GROUP):
            bs = range(g0, g0 + SCAN_GROUP)
            sk = _seg_bf16(jnp.concatenate([(S_scr[b] * row(4, b)).astype(BF16) for b in bs], axis=0), bo)
            lhs = []
            for b in bs:
                lhs.append((row(5, b) * mk_ref[:half]).astype(BF16))
                lhs.append((row(8, b) * mk_ref[half:]).astype(BF16))
            vc = _seg_bf16(jnp.concatenate(lhs, axis=0), bo)
            for i, b in enumerate(bs):
                S_scr[b] = S_scr[b] * row(1, b) + blk(sk, i) * row(3, b) + blk(vc, i) * row(2, b)
        return carry

    lax.fori_loop(0, tb, step, 0, unroll=4)
    emit_out(tb - 1)

    for h in range(N_HEADS):
        o_scr[:, h * HEAD_DIM:(h + 1) * HEAD_DIM] = o8_scr[pl.ds(h, rows, stride=N_HEADS), :]

    o = _head_norm(o_scr[...], bo, RWKV_GN_EPS) * lng_ref[...] + lnb_ref[...]
    ha = (o + seq_scr[7]) * seq_scr[6]
    if sample:
        ha_ref[...] = ha
    else:
        for b in range(nb):
            ha_ref[b] = ha[b * tb:(b + 1) * tb]

    @pl.when(j == pl.num_programs(1) - 1)
    def _():
        for b in range(nb):
            for h in range(N_HEADS):
                sT_ref[b, h] = S_scr[b, :, h * HEAD_DIM:(h + 1) * HEAD_DIM]
            shift_ref[b:b + 1, :] = x[b * tb + tb - 1:b * tb + tb, :]


def _rwkv(P, row0, B, T, tb, fix, s0, lp, bo, sample):
    nb = SCAN_BATCH
    cb = COL_RW // 2048
    if sample:
        assert tb == T
        p_specs = [pl.BlockSpec((nb * T, 2048), lambda i, j: (row0 // (nb * T) + i, cb))]
        fix_spec = pl.BlockSpec((nb * T, RWKV_PROJ), lambda i, j: (i, 0))
        ha_spec = pl.BlockSpec((nb * T, BR_W), lambda i, j: (i, 0))
        ha_shape = jax.ShapeDtypeStruct((B * T, BR_W), F32)
    else:
        p_specs = [pl.BlockSpec((tb, 2048), lambda i, j, b=b: (row0 // tb + (i * nb + b) * (T // tb) + j, cb))
                   for b in range(nb)]
        fix_spec = pl.BlockSpec(fix.shape, lambda i, j: (0, 0))
        ha_spec = pl.BlockSpec((nb, tb, BR_W), lambda i, j: (i, j, 0))
        ha_shape = jax.ShapeDtypeStruct((B, T, BR_W), F32)
    row = lambda n: pl.BlockSpec((1, n), lambda i, j: (0, 0))
    full = lambda a, b: pl.BlockSpec((a, b), lambda i, j: (0, 0))
    st = pl.BlockSpec((nb, N_HEADS, HEAD_DIM, HEAD_DIM), lambda i, j: (i, 0, 0, 0))
    kern = functools.partial(_rwkv_kernel, nb=nb, tb=tb, n_p=len(p_specs), sample=sample)
    return pl.pallas_call(
        kern,
        grid=(B // nb, T // tb),
        in_specs=p_specs + [fix_spec, st, row(RWKV_PROJ), row(BR_W), full(128, BR_W), row(BR_W), full(128, BR_W),
                            full(128, BR_W), row(BR_W), row(BR_W), row(BR_W), full(256, 256),
                            full(HEAD_DIM, BR_W), row(BR_W), row(BR_W)],
        out_specs=[ha_spec, st, pl.BlockSpec((nb, RWKV_PROJ), lambda i, j: (i, 0))],
        out_shape=[ha_shape, jax.ShapeDtypeStruct((B, N_HEADS, HEAD_DIM, HEAD_DIM), F32),
                   jax.ShapeDtypeStruct((B, RWKV_PROJ), F32)],
        scratch_shapes=[pltpu.VMEM((nb, HEAD_DIM, BR_W), F32), pltpu.VMEM((9, nb * tb, BR_W), F32),
                        pltpu.VMEM((nb * tb, BR_W), F32), pltpu.VMEM((nb, RWKV_PROJ), F32),
                        pltpu.VMEM((nb * tb * N_HEADS, HEAD_DIM), F32)],
        compiler_params=_cparams(("parallel", "arbitrary")),
        name="rwkv",
    )(*([P] * len(p_specs)), fix, s0, lp['mu'], lp['w0'], lp['w2'], lp['a0'], lp['a2'], lp['g2'], lp['kk'],
      lp['ka'], lp['rk'], bo, _diag_mask(), lp['ln_g'], lp['ln_b'])


def _diag_mask():
    u = jnp.arange(HEAD_DIM)[:, None]
    j = jnp.arange(BR_W)[None, :] % HEAD_DIM
    return ((j == u) | (j == (u + HEAD_DIM // 2) % HEAD_DIM)).astype(F32)


def _pair_masks():
    lane = lax.broadcasted_iota(jnp.int32, (1, 128), 1)
    m0 = (lane < HEAD_DIM).astype(F32)
    return m0, 1.0 - m0


def _swap_halves(x):
    lane = lax.broadcasted_iota(jnp.int32, x.shape, 1)
    n = x.shape[1]
    return jnp.where((lane & 63) < 32, pltpu.roll(x, n - 32, axis=1), pltpu.roll(x, 32, axis=1))


def _ret_kernel(q_ref, k_ref, v_ref, z_ref, cos_ref, sin_ref, dm_ref, qin_ref, kend_ref, gs_ref, bd_ref, s0_ref,
                bo_ref, gn_ref, h_ref, sT_ref, S_scr, *, nseq, L):
    c = pl.program_id(1)

    @pl.when(c == 0)
    def _():
        _load_pairs(S_scr, s0_ref, nseq)

    cos = cos_ref[...]
    sin = sin_ref[...]
    q = q_ref[...]
    k = k_ref[...]
    q = q * cos + _swap_halves(q) * sin
    k = (k * cos + _swap_halves(k) * sin) * (HEAD_DIM ** -0.5)
    v = v_ref[...]
    qin = qin_ref[...]
    ke = k * kend_ref[...]
    m0, m1 = _pair_masks()
    rows = nseq * L
    rowi = lax.broadcasted_iota(jnp.int32, (rows, 1), 0)
    outs = []
    for p in range(N_HEADS // 2):
        sl = slice(128 * p, 128 * (p + 1))
        qp, kp, vp, kep = q[:, sl], k[:, sl], v[:, sl], ke[:, sl]
        kb = kp.astype(BF16)
        o = jnp.zeros((rows, 128), F32)
        for e, me in enumerate((m0, m1)):
            sc = _dot_nt((qp * me).astype(BF16), kb) * dm_ref[2 * p + e]
            o = o + _dot(sc.astype(BF16), (vp * me).astype(BF16))
        qb = qp.astype(BF16)
        for s in range(nseq):
            S = S_scr[s, p]
            oi = _dot(qb, S.astype(BF16)) * qin[:, sl]
            if nseq == 1:
                o = o + oi
                kes = kep
            else:
                inseq = (rowi >= s * L) & (rowi < (s + 1) * L)
                o = o + jnp.where(inseq, oi, 0.0)
                kes = jnp.where(inseq, kep, 0.0)
            S_scr[s, p] = S * gs_ref[p] + bd_ref[...] * _dot_tn(kes.astype(BF16), vp.astype(BF16))
        outs.append(o)
    o = jnp.concatenate(outs, axis=1)
    o = _head_norm(o, bo_ref[...], GN_EPS) * gn_ref[...]
    z = z_ref[...]
    h_ref[...] = z * _sigmoid(z) * o

    @pl.when(c == pl.num_programs(1) - 1)
    def _():
        _store_pairs(sT_ref, S_scr, nseq)


def _load_pairs(S_scr, s0_ref, nseq):
    S_scr[...] = jnp.zeros_like(S_scr)
    for s in range(nseq):
        for p in range(N_HEADS // 2):
            S_scr[s, p, 0:HEAD_DIM, 0:HEAD_DIM] = s0_ref[s, 2 * p]
            S_scr[s, p, HEAD_DIM:, HEAD_DIM:] = s0_ref[s, 2 * p + 1]


def _store_pairs(sT_ref, S_scr, nseq):
    for s in range(nseq):
        for p in range(N_HEADS // 2):
            sT_ref[s, 2 * p] = S_scr[s, p, 0:HEAD_DIM, 0:HEAD_DIM]
            sT_ref[s, 2 * p + 1] = S_scr[s, p, HEAD_DIM:, HEAD_DIM:]


def _retention(P, row0, n_groups, n_chunks, nseq, L, tabs, s0, bo, gn):
    rows = nseq * L
    rb0 = row0 // rows
    cos, sin, dm, qin, kend, gs, bd = tabs
    col = lambda cb: pl.BlockSpec((rows, BR_W), lambda i, c: (rb0 + i * n_chunks + c, cb))
    tab = pl.BlockSpec((rows, BR_W), lambda i, c: (c, 0))
    cst = lambda shp: pl.BlockSpec(shp, lambda i, c: (0,) * len(shp))
    st = pl.BlockSpec((nseq, N_HEADS, HEAD_DIM, HEAD_DIM), lambda i, c: (i, 0, 0, 0))
    kern = functools.partial(_ret_kernel, nseq=nseq, L=L)
    return pl.pallas_call(
        kern,
        grid=(n_groups, n_chunks),
        in_specs=[col(0), col(1), col(2), col(3), tab, tab, cst((N_HEADS, rows, rows)), cst((rows, BR_W)),
                  cst((rows, BR_W)), cst((4, 128, 128)), cst((128, 128)), st, cst((256, 256)), cst((1, BR_W))],
        out_specs=[pl.BlockSpec((rows, BR_W), lambda i, c: (i * n_chunks + c, 0)), st],
        out_shape=[jax.ShapeDtypeStruct((n_groups * n_chunks * rows, BR_W), F32),
                   jax.ShapeDtypeStruct((n_groups * nseq, N_HEADS, HEAD_DIM, HEAD_DIM), F32)],
        scratch_shapes=[pltpu.VMEM((nseq, 4, 128, 128), F32)],
        compiler_params=_cparams(("parallel", "arbitrary")),
        name="retention",
    )(P, P, P, P, cos, sin, dm, qin, kend, gs, bd, s0, bo, gn)


def _mlstm_kernel(qk_ref, v_ref, o_ref, if_ref, xcs_ref, cw_ref, cb_ref, bi_ref, bf_ref, tri_ref, bd_ref, eye_ref,
                  c0_ref, n0_ref, m0_ref, bo_ref, gn_ref, h_ref, cT_ref, nT_ref, mT_ref, convT_ref,
                  C_scr, n_scr, m_scr, xc_scr, *, nseq, L):
    c = pl.program_id(1)
    rows = nseq * L

    @pl.when(c == 0)
    def _():
        _load_pairs(C_scr, c0_ref, nseq)
        n_scr[...] = n0_ref[...]
        m_scr[...] = m0_ref[...]
        if nseq == 1:
            xc_scr[0:8, :] = jnp.zeros((8, 2 * BR_W), F32)

    cw = cw_ref[...]
    if nseq == 1:
        xc_scr[8:8 + L, :] = qk_ref[...]
        conv = cb_ref[...]
        for w in range(CONV_W):
            conv = conv + xc_scr[5 + w:5 + w + L, :] * cw[w:w + 1, :]
        xc_scr[0:8, :] = xc_scr[L:L + 8, :]
    else:
        parts = []
        for s in range(nseq):
            cs = cb_ref[...]
            for w in range(CONV_W):
                cs = cs + xcs_ref[s, 1 + w:1 + w + L, :] * cw[w:w + 1, :]
            parts.append(cs)
        conv = jnp.concatenate(parts, axis=0)
    qk = conv * _sigmoid(conv)
    q = qk[:, :BR_W]
    k = qk[:, BR_W:] * (HEAD_DIM ** -0.5)
    v = v_ref[...]

    rowi = lax.broadcasted_iota(jnp.int32, (rows, 1), 0)
    tok = rowi % L
    ig = if_ref[:, 0:128] + bi_ref[...]
    fp = if_ref[:, 128:256] + bf_ref[...]
    lf = jnp.minimum(fp, 0.0) - jnp.log(1.0 + jnp.exp(-jnp.abs(fp)))
    tri = tri_ref[...]
    l1, l2, l3 = _split3(lf)
    bcum = _dot(tri, l1) + _dot(tri, l2) + _dot(tri, l3)
    u = ig - bcum
    cm = u
    sh = 1
    while sh < L:
        cm = jnp.maximum(cm, jnp.where(tok >= sh, pltpu.roll(cm, sh, axis=0), -jnp.inf))
        sh *= 2
    if nseq == 1:
        m0r = m_scr[0]
    else:
        m0r = jnp.concatenate([jnp.broadcast_to(m_scr[s], (L, 128)) for s in range(nseq)], axis=0)
    mt = bcum + jnp.maximum(m0r, cm)
    inter = jnp.exp(bcum + m0r - mt)
    bmm = bcum - mt
    emt = jnp.exp(-mt)
    wend_parts = []
    for s in range(nseq):
        last = slice(s * L + L - 1, s * L + L)
        m_end = mt[last, :]
        wend_parts.append(jnp.exp(bcum[last, :] - m_end + u[s * L:(s + 1) * L, :]))
    wend = wend_parts[0] if nseq == 1 else jnp.concatenate(wend_parts, axis=0)
    eye = eye_ref[...]
    u1, u2, u3 = _split3(u)
    ut = _dot_nt(eye, u1) + _dot_nt(eye, u2) + _dot_nt(eye, u3)
    trib = tri > 0.5

    m0m, m1m = _pair_masks()
    lane128 = lax.broadcasted_iota(jnp.int32, (1, 128), 1)
    first = lane128 < HEAD_DIM
    bo = bo_ref[...]
    qn_all = None
    outs = []
    wk_all = []
    for p in range(N_HEADS // 2):
        sl = slice(128 * p, 128 * (p + 1))
        qp, kp, vp = q[:, sl], k[:, sl], v[:, sl]
        kb = kp.astype(BF16)
        num = jnp.zeros((rows, 128), F32)
        dsum = []
        for e, me in enumerate((m0m, m1m)):
            h = 2 * p + e
            logd = bmm[:, h:h + 1] + ut[h:h + 1, :]
            dmat = jnp.where(trib, jnp.exp(jnp.minimum(logd, 0.0)), 0.0)
            sc = _dot_nt((qp * me).astype(BF16), kb) * dmat
            num = num + _dot(sc.astype(BF16), (vp * me).astype(BF16))
            dsum.append(jnp.sum(sc, axis=1, keepdims=True))
        pick = lambda arr: jnp.where(first, arr[:, 2 * p:2 * p + 1], arr[:, 2 * p + 1:2 * p + 2])
        inter_p = pick(inter)
        den = jnp.where(first, dsum[0], dsum[1])
        wend_p = pick(wend)
        qb = qp.astype(BF16)
        kw = kp * wend_p
        wk_all.append(kw)
        qn = jnp.zeros((rows, 128), F32)
        for s in range(nseq):
            C = C_scr[s, p]
            nrow = n_scr[s][:, sl]
            qc = _dot(qb, C.astype(BF16))
            qns = qp * nrow
            if nseq == 1:
                num = num + inter_p * qc
                qn = qns
                kws = kw
                send = inter_p[L - 1:L, :]
            else:
                inseq = (rowi >= s * L) & (rowi < (s + 1) * L)
                num = num + jnp.where(inseq, inter_p * qc, 0.0)
                qn = qn + jnp.where(inseq, qns, 0.0)
                kws = jnp.where(inseq, kw, 0.0)
                send = inter_p[s * L + L - 1:s * L + L, :]
            send_col = jnp.where(lax.broadcasted_iota(jnp.int32, (128, 1), 0) < HEAD_DIM,
                                 send[:, 0:1], send[:, 64:65])
            C_scr[s, p] = C * send_col + bd_ref[...] * _dot_tn(kws.astype(BF16), vp.astype(BF16))
        den = den + inter_p * _segsum_pair(qn, bo)
        outs.append(num / jnp.maximum(jnp.abs(den), pick(emt)))
    hh = jnp.concatenate(outs, axis=1)
    kw_full = jnp.concatenate(wk_all, axis=1)
    for s in range(nseq):
        last = slice(s * L + L - 1, s * L + L)
        send_row = _expand_heads(inter[last, :])
        n_scr[s] = send_row * n_scr[s] + jnp.sum(kw_full[s * L:(s + 1) * L, :], axis=0, keepdims=True)
        m_scr[s] = mt[last, :]
    op = o_ref[...]
    hh = _sigmoid(op) * hh
    h_ref[...] = _head_norm(hh, bo, GN_EPS) * gn_ref[...]

    @pl.when(c == pl.num_programs(1) - 1)
    def _():
        _store_pairs(cT_ref, C_scr, nseq)
        nT_ref[...] = n_scr[...]
        mT_ref[...] = m_scr[...]
        if nseq == 1:
            convT_ref[0] = xc_scr[8 - (CONV_W - 1):8, :]
        else:
            for s in range(nseq):
                convT_ref[s] = xcs_ref[s, 8 - (CONV_W - 1):8, :]


def _segsum_pair(x, bo):
    hi, lo = _split2(x)
    b2 = bo[:128, :128]
    return _dot(hi, b2) + _dot(lo, b2)


def _expand_heads(row):
    lane = lax.broadcasted_iota(jnp.int32, (1, BR_W), 1)
    out = jnp.zeros((1, BR_W), F32)
    for h in range(N_HEADS):
        out = jnp.where((lane >> 6) == h, row[:, h:h + 1], out)
    return out


def _mlstm(P, row0, n_groups, n_chunks, nseq, L, xcs, lp, tabs, c0, n0, m0, bo):
    rows = nseq * L
    rb0 = row0 // rows
    tri, bd, eye = tabs
    rowblk = lambda width, cb: pl.BlockSpec((rows, width), lambda i, c: (rb0 + i * n_chunks + c, cb))
    cst = lambda shp: pl.BlockSpec(shp, lambda i, c: (0,) * len(shp))
    stC = pl.BlockSpec((nseq, N_HEADS, HEAD_DIM, HEAD_DIM), lambda i, c: (i, 0, 0, 0))
    stn = pl.BlockSpec((nseq, 1, BR_W), lambda i, c: (i, 0, 0))
    stm = pl.BlockSpec((nseq, 1, 128), lambda i, c: (i, 0, 0))
    if nseq == 1:
        xcs_spec = cst((1, 8, 2 * BR_W))
    else:
        xcs_spec = pl.BlockSpec((nseq, 8, 2 * BR_W), lambda i, c: (i, 0, 0))
    kern = functools.partial(_mlstm_kernel, nseq=nseq, L=L)
    nB = n_groups * nseq
    return pl.pallas_call(
        kern,
        grid=(n_groups, n_chunks),
        in_specs=[rowblk(2 * BR_W, COL_ML // (2 * BR_W)), rowblk(BR_W, (COL_ML + 1024) // BR_W),
                  rowblk(BR_W, (COL_ML + 1536) // BR_W), rowblk(256, COL_IF // 256), xcs_spec,
                  cst((CONV_W, 2 * BR_W)), cst((1, 2 * BR_W)), cst((1, 128)), cst((1, 128)),
                  cst((rows, rows)), cst((128, 128)), cst((128, 128)), stC, stn, stm, cst((256, 256)),
                  cst((1, BR_W))],
        out_specs=[pl.BlockSpec((rows, BR_W), lambda i, c: (i * n_chunks + c, 0)), stC, stn, stm,
                   pl.BlockSpec((nseq, CONV_W - 1, 2 * BR_W), lambda i, c: (i, 0, 0))],
        out_shape=[jax.ShapeDtypeStruct((n_groups * n_chunks * rows, BR_W), F32),
                   jax.ShapeDtypeStruct((nB, N_HEADS, HEAD_DIM, HEAD_DIM), F32),
                   jax.ShapeDtypeStruct((nB, 1, BR_W), F32),
                   jax.ShapeDtypeStruct((nB, 1, 128), F32),
                   jax.ShapeDtypeStruct((nB, CONV_W - 1, 2 * BR_W), F32)],
        scratch_shapes=[pltpu.VMEM((nseq, 4, 128, 128), F32), pltpu.VMEM((nseq, 1, BR_W), F32),
                        pltpu.VMEM((nseq, 1, 128), F32), pltpu.VMEM((L + 8, 2 * BR_W), F32)],
        compiler_params=_cparams(("parallel", "arbitrary")),
        name="mlstm",
    )(P, P, P, P, xcs, lp['conv_w'], lp['conv_b'], lp['b_i'], lp['b_f'], tri, bd, eye, c0, n0, m0, bo,
      lp['mlstm_gn'])


def _mix_kernel(x_ref, hap_ref, has_ref, hrp_ref, hrs_ref, hmp_ref, hms_ref, g0_ref, g1_ref, g2_ref, wb_ref,
                bg_ref, wo_ref, o_ref, *, n_p):
    def body(h_refs):
        mixed = None
        for n, (h_ref, pg_ref) in enumerate(zip(h_refs, (g0_ref, g1_ref, g2_ref))):
            ub = _dot(h_ref[...].astype(BF16), wb_ref[n])
            t = _sigmoid(pg_ref[...] + bg_ref[n]) * ub
            mixed = t if mixed is None else mixed + t
        o_ref[...] = x_ref[...] + _dot(mixed.astype(BF16), wo_ref[...])

    @pl.when(pl.program_id(0) < n_p)
    def _():
        body((hap_ref, hrp_ref, hmp_ref))

    @pl.when(pl.program_id(0) >= n_p)
    def _():
        body((has_ref, hrs_ref, hms_ref))


def _mix(x, hp, hs, P, wb, bg, wo, tm):
    R = x.shape[0]
    n_p = hp[0].shape[0] // tm
    xs = pl.BlockSpec((tm, D_MODEL), lambda i: (i, 0))
    hps = pl.BlockSpec((tm, BR_W), lambda i: (jnp.minimum(i, n_p - 1), 0))
    hss = pl.BlockSpec((tm, BR_W), lambda i: (jnp.maximum(i - n_p, 0), 0))
    gs = lambda n: pl.BlockSpec((tm, D_MODEL), lambda i: (i, COL_GATE // D_MODEL + n))
    return pl.pallas_call(
        functools.partial(_mix_kernel, n_p=n_p),
        grid=(R // tm,),
        in_specs=[xs, hps, hss, hps, hss, hps, hss, gs(0), gs(1), gs(2),
                  pl.BlockSpec((3, BR_W, D_MODEL), lambda i: (0, 0, 0)),
                  pl.BlockSpec((3, 1, D_MODEL), lambda i: (0, 0, 0)),
                  pl.BlockSpec((D_MODEL, D_MODEL), lambda i: (0, 0))],
        out_specs=xs,
        out_shape=jax.ShapeDtypeStruct((R, D_MODEL), F32),
        compiler_params=_cparams(("parallel",)),
        name="mix_out",
    )(x, hp[0], hs[0], hp[1], hs[1], hp[2], hs[2], P, P, P, wb, bg, wo)


def _ffn_kernel(x_ref, g_ref, w1_ref, w2_ref, gf_ref, o_ref, xn_scr, acc_scr, *, final):
    j = pl.program_id(1)

    @pl.when(j == 0)
    def _():
        xn_scr[...] = _rms(x_ref[...], g_ref[...]).astype(BF16)
        acc_scr[...] = jnp.zeros_like(acc_scr)

    h = jnp.maximum(_dot(xn_scr[...], w1_ref[...]), 0.0)
    acc_scr[...] += _dot((h * h).astype(BF16), w2_ref[...])

    @pl.when(j == pl.num_programs(1) - 1)
    def _():
        y = x_ref[...] + acc_scr[...]
        if final:
            y = _rms(y, gf_ref[...])
        o_ref[...] = y


def _ffn(x, g, w1, w2, gf, tm, tf, final):
    R = x.shape[0]
    kern = functools.partial(_ffn_kernel, final=final)
    return pl.pallas_call(
        kern,
        grid=(R // tm, D_FF // tf),
        in_specs=[pl.BlockSpec((tm, D_MODEL), lambda i, j: (i, 0)),
                  pl.BlockSpec((1, D_MODEL), lambda i, j: (0, 0)),
                  pl.BlockSpec((D_MODEL, tf), lambda i, j: (0, j)),
                  pl.BlockSpec((tf, D_MODEL), lambda i, j: (j, 0)),
                  pl.BlockSpec((1, D_MODEL), lambda i, j: (0, 0))],
        out_specs=pl.BlockSpec((tm, D_MODEL), lambda i, j: (i, 0)),
        out_shape=jax.ShapeDtypeStruct((R, D_MODEL), F32),
        scratch_shapes=[pltpu.VMEM((tm, D_MODEL), BF16), pltpu.VMEM((tm, D_MODEL), F32)],
        compiler_params=_cparams(("parallel", "arbitrary")),
        name="ffn",
    )(x, g, w1, w2, gf)


def _rope_tables(T, pos0):
    half = HEAD_DIM // 2
    inv = jnp.power(ROPE_BASE, -jnp.arange(half, dtype=F32) / half)
    ang = (jnp.arange(T, dtype=F32) + pos0)[:, None] * inv[None, :]
    cos, sin = jnp.cos(ang), jnp.sin(ang)
    cos_h = jnp.concatenate([cos, cos], axis=1)
    sin_h = jnp.concatenate([-sin, sin], axis=1)
    return jnp.tile(cos_h, (1, N_HEADS)), jnp.tile(sin_h, (1, N_HEADS))


def _ret_tables(nseq, L):
    log_g = jnp.log(1.0 - jnp.exp2(-5.0 - jnp.arange(N_HEADS, dtype=F32)))
    idx = jnp.arange(L, dtype=F32)
    diff = idx[:, None] - idx[None, :]
    dmask = jnp.where(diff[None] >= 0, jnp.exp(jnp.maximum(diff, 0.0)[None] * log_g[:, None, None]), 0.0)
    q_in = jnp.exp((idx + 1.0)[:, None] * log_g[None, :])
    k_end = jnp.exp((L - 1.0 - idx)[:, None] * log_g[None, :])
    g_chunk = jnp.exp(L * log_g)
    eye = jnp.eye(nseq, dtype=F32)
    dm = jnp.einsum('st,hij->hsitj', eye, dmask).reshape(N_HEADS, nseq * L, nseq * L)
    qin = jnp.tile(jnp.repeat(q_in, HEAD_DIM, axis=1), (nseq, 1))
    kend = jnp.tile(jnp.repeat(k_end, HEAD_DIM, axis=1), (nseq, 1))
    gs = jnp.broadcast_to(jnp.repeat(g_chunk, HEAD_DIM).reshape(4, 128, 1), (4, 128, 128))
    return dm, qin, kend, gs


def _block_diag_ones(n):
    i = jnp.arange(n) // HEAD_DIM
    return (i[:, None] == i[None, :])


def _seq_tri(nseq, L):
    r = jnp.arange(nseq * L)
    return ((r[:, None] // L == r[None, :] // L) & (r[:, None] >= r[None, :])).astype(BF16)


def _pick_tile(n, cands):
    for c in cands:
        if n % c == 0:
            return c
    raise ValueError(f"no tile for {n}")


def kernel(x_prompt, x_sample, state_rwkv_wkv, state_rwkv_shift, state_ret, state_mlstm_C, state_mlstm_n,
           state_mlstm_m, state_mlstm_conv, norm_mix, w_in, rwkv_mu, rwkv_w0, rwkv_w2, rwkv_a0, rwkv_a2,
           rwkv_g2, rwkv_kk, rwkv_ka, rwkv_rk, rwkv_ln_g, rwkv_ln_b, ret_gn_g, mlstm_conv_w, mlstm_conv_b,
           mlstm_b_i, mlstm_b_f, mlstm_gn_g, w_branch, b_gate, w_out, norm_ffn, w_ff1, w_ff2, norm_final):
    Bp, Tp, _ = x_prompt.shape
    Bs, Ts, _ = x_sample.shape
    Rp, Rs = Bp * Tp, Bs * Ts
    R = Rp + Rs
    past_len = 16384
    Lp = min(PROMPT_CHUNK, Tp)
    assert Tp % Lp == 0 and Bs % SAMPLE_SEQS == 0 and Rp % (SAMPLE_SEQS * Ts) == 0
    assert Tp & (Tp - 1) == 0 and Ts & (Ts - 1) == 0 and Tp >= CONV_W - 1 and Bp % SCAN_BATCH == 0

    x = jnp.concatenate([x_prompt.reshape(Rp, D_MODEL), x_sample.reshape(Rs, D_MODEL)], axis=0)

    o_rw, o_ret, o_ml = 0, RWKV_PROJ, RWKV_PROJ + 4 * BR_W
    o_if = o_ml + 4 * BR_W
    o_gate = o_if + 2 * N_HEADS
    zpad = lambda n: jnp.zeros((DEPTH, D_MODEL, n), F32)
    w_cat = jnp.concatenate([
        w_in[:, :, o_ret:o_ret + 4 * BR_W], w_in[:, :, o_ml:o_ml + 4 * BR_W], w_in[:, :, o_rw:o_rw + RWKV_PROJ],
        w_in[:, :, o_if:o_if + N_HEADS], zpad(128 - N_HEADS), w_in[:, :, o_if + N_HEADS:o_gate],
        zpad(128 - N_HEADS), w_in[:, :, o_gate:]], axis=2).astype(BF16)
    pad_rows = lambda a, top: jnp.concatenate(
        [jnp.zeros((DEPTH, top, BR_W), F32), a, jnp.zeros((DEPTH, 128 - top - a.shape[1], BR_W), F32)], axis=1)
    w2p = pad_rows(rwkv_w2, 0).astype(BF16)
    a2p = pad_rows(rwkv_a2, W_LORA).astype(BF16)
    g2b = rwkv_g2.astype(BF16)
    pad128 = lambda a: jnp.concatenate([a, jnp.zeros((DEPTH, 128 - N_HEADS), F32)], axis=1)
    wbb, wob, w1b, w2b = (t.astype(BF16) for t in (w_branch, w_out, w_ff1, w_ff2))
    bo = _block_diag_ones(256).astype(BF16)
    bd = _block_diag_ones(128).astype(F32)
    eye = jnp.eye(128, dtype=BF16)

    cos_p, sin_p = _rope_tables(Tp, 0)
    cos_s, sin_s = _rope_tables(Ts, past_len)
    cos_s, sin_s = jnp.tile(cos_s, (SAMPLE_SEQS, 1)), jnp.tile(sin_s, (SAMPLE_SEQS, 1))
    rt_p = _ret_tables(1, Lp)
    rt_s = _ret_tables(SAMPLE_SEQS, Ts)
    tri_p, tri_s = _seq_tri(1, Lp), _seq_tri(SAMPLE_SEQS, Ts)

    tm_in = _pick_tile(R, (1536, 768, 576, 192, 64))
    tm_mix = _pick_tile(math.gcd(Rp, Rs), (512, 64))
    tb_p = min(SCAN_TBLK, Tp)

    zeros_heads = jnp.zeros((Bp, N_HEADS, HEAD_DIM, HEAD_DIM), F32)
    outs = [[] for _ in range(14)]
    for l in range(DEPTH):
        lp = dict(mu=rwkv_mu[l][None], w0=rwkv_w0[l][None], w2=w2p[l], a0=rwkv_a0[l][None], a2=a2p[l], g2=g2b[l],
                  kk=rwkv_kk[l][None], ka=rwkv_ka[l][None], rk=rwkv_rk[l].reshape(1, BR_W),
                  ln_g=rwkv_ln_g[l][None], ln_b=rwkv_ln_b[l][None],
                  conv_w=mlstm_conv_w[l], conv_b=mlstm_conv_b[l][None], b_i=pad128(mlstm_b_i)[l][None],
                  b_f=pad128(mlstm_b_f)[l][None], mlstm_gn=mlstm_gn_g[l][None])
        P = _inproj(x, norm_mix[l][None], w_cat[l], tm_in, 1536)

        fix_s = jnp.repeat(state_rwkv_shift[l], Ts, axis=0)
        ha_p, wkv_p, shift_p = _rwkv(P, 0, Bp, Tp, tb_p, jnp.zeros((8, 128), F32), zeros_heads, lp, bo, False)
        ha_s, wkv_s, shift_s = _rwkv(P, Rp, Bs, Ts, Ts, fix_s, state_rwkv_wkv[l], lp, bo, True)
        ha_p = ha_p.reshape(Rp, BR_W)

        gn = ret_gn_g[l][None]
        hr_p, ret_p = _retention(P, 0, Bp, Tp // Lp, 1, Lp, (cos_p, sin_p) + rt_p + (bd,), zeros_heads, bo, gn)
        hr_s, ret_s = _retention(P, Rp, Bs // SAMPLE_SEQS, 1, SAMPLE_SEQS, Ts, (cos_s, sin_s) + rt_s + (bd,),
                                 state_ret[l], bo, gn)

        qk_s = lax.slice(P, (Rp, COL_ML), (R, COL_ML + 2 * BR_W)).reshape(Bs, Ts, 2 * BR_W)
        xcs = jnp.concatenate([jnp.zeros((Bs, 1, 2 * BR_W), F32), state_mlstm_conv[l], qk_s], axis=1)
        hm_p, c_p, n_p, m_p, conv_p = _mlstm(P, 0, Bp, Tp // Lp, 1, Lp, jnp.zeros((1, 8, 2 * BR_W), F32), lp,
                                             (tri_p, bd, eye), zeros_heads, jnp.zeros((Bp, 1, BR_W), F32),
                                             jnp.zeros((Bp, 1, 128), F32), bo)
        m0_s = jnp.concatenate([state_mlstm_m[l], jnp.zeros((Bs, 128 - N_HEADS), F32)], axis=1)[:, None]
        hm_s, c_s, n_s, m_s, conv_s = _mlstm(P, Rp, Bs // SAMPLE_SEQS, 1, SAMPLE_SEQS, Ts, xcs, lp,
                                             (tri_s, bd, eye), state_mlstm_C[l],
                                             state_mlstm_n[l].reshape(Bs, 1, BR_W), m0_s, bo)

        x = _mix(x, (ha_p, hr_p, hm_p), (ha_s, hr_s, hm_s), P, wbb[l], b_gate[l][:, None], wob[l], tm_mix)
        x = _ffn(x, norm_ffn[l][None], w1b[l], w2b[l], norm_final[None], tm_in, 512, l == DEPTH - 1)

        for i, t in enumerate((wkv_p, shift_p, ret_p, c_p, n_p.reshape(Bp, N_HEADS, HEAD_DIM),
                               m_p[:, 0, :N_HEADS], conv_p,
                               wkv_s, shift_s, ret_s, c_s, n_s.reshape(Bs, N_HEADS, HEAD_DIM),
                               m_s[:, 0, :N_HEADS], conv_s)):
            outs[i].append(t)

    y_p = x[:Rp].reshape(Bp, Tp, D_MODEL)
    y_s = x[Rp:].reshape(Bs, Ts, D_MODEL)
    return (y_p, y_s) + tuple(jnp.stack(o, axis=0) for o in outs)
```

```python
import functools
import math

import jax
import jax.numpy as jnp
from jax import lax
from jax.experimental import pallas as pl
from jax.experimental.pallas import tpu as pltpu

F32 = jnp.float32
BF16 = jnp.bfloat16

D_MODEL = 1024
DEPTH = 4
HEAD_DIM = 64
N_HEADS = 8
BR_W = N_HEADS * HEAD_DIM
W_LORA = 64
A_LORA = 64
G_LORA = 128
CONV_W = 4
D_FF = 4 * D_MODEL
ROPE_BASE = 10000.0
RMS_EPS = 1e-6
RWKV_GN_EPS = 64e-5
GN_EPS = 1e-5
RWKV_PROJ = 3 * BR_W + W_LORA + A_LORA + G_LORA

COL_RET = 0
COL_ML = 2048
COL_RW = 4096
COL_IF = COL_RW + RWKV_PROJ
COL_GATE = 6144
N_PROJ = COL_GATE + 3 * D_MODEL

PROMPT_CHUNK = 128
SAMPLE_SEQS = 8
SCAN_BATCH = 8
SCAN_TBLK = 64
SCAN_SUB = 8
VMEM_LIMIT = 56 * 1024 * 1024


def _cparams(sem):
    return pltpu.CompilerParams(dimension_semantics=sem, vmem_limit_bytes=VMEM_LIMIT)


def _dot(a, b):
    return jnp.dot(a, b, preferred_element_type=F32)


def _dot_nt(a, b):
    return lax.dot_general(a, b, (((1,), (1,)), ((), ())), preferred_element_type=F32)


def _dot_tn(a, b):
    return lax.dot_general(a, b, (((0,), (0,)), ((), ())), preferred_element_type=F32)


def _split2(x):
    hi = x.astype(BF16)
    lo = (x - hi.astype(F32)).astype(BF16)
    return hi, lo


def _split3(x):
    x1 = x.astype(BF16)
    r1 = x - x1.astype(F32)
    x2 = r1.astype(BF16)
    x3 = (r1 - x2.astype(F32)).astype(BF16)
    return x1, x2, x3


def _sigmoid(x):
    return 1.0 / (1.0 + jnp.exp(-x))


def _softplus(x):
    return jnp.maximum(x, 0.0) + jnp.log(1.0 + jnp.exp(-jnp.abs(x)))


def _seg_bf16(xb, bo):
    return jnp.concatenate([_dot(xb[:, :256], bo), _dot(xb[:, 256:], bo)], axis=1)


def _segsum(x, bo):
    hi, lo = _split2(x)
    return _seg_bf16(hi, bo) + _seg_bf16(lo, bo)


def _head_norm(h, bo, eps):
    mu = _segsum(h, bo) * (1.0 / HEAD_DIM)
    d = h - mu
    var = _segsum(d * d, bo) * (1.0 / HEAD_DIM)
    return d * lax.rsqrt(var + eps)


def _rms(x, g):
    return x * lax.rsqrt(jnp.mean(x * x, axis=-1, keepdims=True) + RMS_EPS) * g


def _inproj_kernel(x_ref, g_ref, w_ref, o_ref, xn_scr):
    @pl.when(pl.program_id(1) == 0)
    def _():
        xn_scr[...] = _rms(x_ref[...], g_ref[...]).astype(BF16)

    o_ref[...] = _dot(xn_scr[...], w_ref[...])


def _inproj(x, g, w, tm, tn):
    R = x.shape[0]
    return pl.pallas_call(
        _inproj_kernel,
        grid=(R // tm, N_PROJ // tn),
        in_specs=[pl.BlockSpec((tm, D_MODEL), lambda i, j: (i, 0)),
                  pl.BlockSpec((1, D_MODEL), lambda i, j: (0, 0)),
                  pl.BlockSpec((D_MODEL, tn), lambda i, j: (0, j))],
        out_specs=pl.BlockSpec((tm, tn), lambda i, j: (i, j)),
        out_shape=jax.ShapeDtypeStruct((R, N_PROJ), F32),
        scratch_shapes=[pltpu.VMEM((tm, D_MODEL), BF16)],
        compiler_params=_cparams(("parallel", "arbitrary")),
        name="inproj",
    )(x, g, w)


def _rwkv_kernel(*refs, nb, tb, n_p, sample):
    p_refs = refs[:n_p]
    (fix_ref, s0_ref, mu_ref, w0_ref, w2_ref, a0_ref, a2_ref, g2_ref, kkp_ref, kap_ref, rk_ref, bo_ref, mk_ref,
     lng_ref, lnb_ref, ha_ref, sT_ref, shift_ref, S_scr, seq_scr, o_scr, carry_scr, o8_scr,
     vc_scr) = refs[n_p:]
    sub = min(SCAN_SUB, tb)
    j = pl.program_id(1)
    rows = nb * tb

    @pl.when(j == 0)
    def _():
        for b in range(nb):
            for h in range(N_HEADS):
                S_scr[b, :, h * HEAD_DIM:(h + 1) * HEAD_DIM] = s0_ref[b, h]
        carry_scr[...] = jnp.zeros_like(carry_scr)

    if sample:
        x = p_refs[0][:, :RWKV_PROJ]
        tok = lax.broadcasted_iota(jnp.int32, (rows, 1), 0) & (tb - 1)
        prev = jnp.where(tok == 0, fix_ref[...], pltpu.roll(x, 1, axis=0))
    else:
        first = lax.broadcasted_iota(jnp.int32, (tb, 1), 0) == 0
        xs = [p_refs[b][:, :RWKV_PROJ] for b in range(nb)]
        prev = jnp.concatenate([jnp.where(first, carry_scr[b:b + 1, :], pltpu.roll(xs[b], 1, axis=0))
                                for b in range(nb)], axis=0)
        for b in range(nb):
            carry_scr[b:b + 1, :] = xs[b][tb - 1:tb, :]
        x = jnp.concatenate(xs, axis=0)
    pm = x + (prev - x) * mu_ref[...]
    r = pm[:, 0:512]
    k = pm[:, 512:1024]
    v = pm[:, 1024:1536]
    wa = pm[:, 1536:1664]
    gd = pm[:, 1664:1792]
    bo = bo_ref[...]
    lw = _dot(jnp.tanh(wa).astype(BF16), w2_ref[...])
    w_log = -_softplus(-(w0_ref[...] + lw)) - 0.5
    a = _sigmoid(a0_ref[...] + _dot(wa.astype(BF16), a2_ref[...]))
    kk = k * kkp_ref[...]
    kk = kk / jnp.maximum(jnp.sqrt(_segsum(kk * kk, bo)), 1e-12)
    kt = k * (1.0 + (a - 1.0) * kap_ref[...])
    seq_scr[0] = r
    seq_scr[1] = jnp.exp(-jnp.exp(w_log))
    seq_scr[2] = kt
    seq_scr[3] = -(kk * a)
    seq_scr[4] = kk
    seq_scr[6] = _dot(_sigmoid(gd).astype(BF16), g2_ref[...])
    seq_scr[7] = _segsum(r * kt * rk_ref[...], bo) * v
    v_hi = v.astype(BF16).astype(F32)
    v_lo = _swap_halves(v - v_hi)
    low_half = (lax.broadcasted_iota(jnp.int32, (1, BR_W), 1) & (HEAD_DIM - 1)) < HEAD_DIM // 2
    seq_scr[5] = jnp.where(low_half, v_hi, v_lo)
    seq_scr[8] = jnp.where(low_half, v_lo, v_hi)

    blk = lambda arr, b: arr[b * HEAD_DIM:(b + 1) * HEAD_DIM]
    half = HEAD_DIM // 2
    head_mask = (lax.broadcasted_iota(jnp.int32, (N_HEADS, BR_W), 1) // HEAD_DIM
                 == lax.broadcasted_iota(jnp.int32, (N_HEADS, BR_W), 0)).astype(F32)

    def emit_out(t, s_bf):
        for b in range(nb):
            r8 = (seq_scr[0, pl.ds(b * tb + t, 1), :] * head_mask).astype(BF16)
            o8 = _dot_nt(r8, s_bf[b])
            o8_scr[pl.ds(pl.multiple_of((b * tb + t) * N_HEADS, N_HEADS), N_HEADS), :] = o8

    def step(t, tt):
        row = lambda q, b: seq_scr[q, pl.ds(b * tb + t, 1), :]
        s_bf = [S_scr[b].astype(BF16) for b in range(nb)]
        emit_out(jnp.maximum(t - 1, 0), s_bf)
        sk = _seg_bf16(jnp.concatenate([s_bf[b] * row(4, b).astype(BF16) for b in range(nb)], axis=0), bo)
        for b in range(nb):
            vc = vc_scr[tt, b * HEAD_DIM:(b + 1) * HEAD_DIM, :]
            S_scr[b] = S_scr[b] * row(1, b) + blk(sk, b) * row(3, b) + vc * row(2, b)

    def sub_block(sb, carry):
        t0 = sb * sub
        lhs = []
        for tt in range(sub):
            for b in range(nb):
                lhs.append((seq_scr[5, pl.ds(b * tb + t0 + tt, 1), :] * mk_ref[:half]).astype(BF16))
                lhs.append((seq_scr[8, pl.ds(b * tb + t0 + tt, 1), :] * mk_ref[half:]).astype(BF16))
        vc_all = _seg_bf16(jnp.concatenate(lhs, axis=0), bo)
        for tt in range(sub):
            vc_scr[tt] = vc_all[tt * nb * HEAD_DIM:(tt + 1) * nb * HEAD_DIM]
        for tt in range(sub):
            step(t0 + tt, tt)
        return carry

    lax.fori_loop(0, tb // sub, sub_block, 0)
    emit_out(tb - 1, [S_scr[b].astype(BF16) for b in range(nb)])

    for h in range(N_HEADS):
        o_scr[:, h * HEAD_DIM:(h + 1) * HEAD_DIM] = o8_scr[pl.ds(h, rows, stride=N_HEADS), :]

    o = _head_norm(o_scr[...], bo, RWKV_GN_EPS) * lng_ref[...] + lnb_ref[...]
    ha = (o + seq_scr[7]) * seq_scr[6]
    if sample:
        ha_ref[...] = ha
    else:
        for b in range(nb):
            ha_ref[b] = ha[b * tb:(b + 1) * tb]

    @pl.when(j == pl.num_programs(1) - 1)
    def _():
        for b in range(nb):
            for h in range(N_HEADS):
                sT_ref[b, h] = S_scr[b, :, h * HEAD_DIM:(h + 1) * HEAD_DIM]
            shift_ref[b:b + 1, :] = x[b * tb + tb - 1:b * tb + tb, :]


def _rwkv(P, row0, B, T, tb, fix, s0, lp, bo, sample):
    nb = SCAN_BATCH
    cb = COL_RW // 2048
    if sample:
        assert tb == T
        p_specs = [pl.BlockSpec((nb * T, 2048), lambda i, j: (row0 // (nb * T) + i, cb))]
        fix_spec = pl.BlockSpec((nb * T, RWKV_PROJ), lambda i, j: (i, 0))
        ha_spec = pl.BlockSpec((nb * T, BR_W), lambda i, j: (i, 0))
        ha_shape = jax.ShapeDtypeStruct((B * T, BR_W), F32)
    else:
        p_specs = [pl.BlockSpec((tb, 2048), lambda i, j, b=b: (row0 // tb + (i * nb + b) * (T // tb) + j, cb))
                   for b in range(nb)]
        fix_spec = pl.BlockSpec(fix.shape, lambda i, j: (0, 0))
        ha_spec = pl.BlockSpec((nb, tb, BR_W), lambda i, j: (i, j, 0))
        ha_shape = jax.ShapeDtypeStruct((B, T, BR_W), F32)
    row = lambda n: pl.BlockSpec((1, n), lambda i, j: (0, 0))
    full = lambda a, b: pl.BlockSpec((a, b), lambda i, j: (0, 0))
    st = pl.BlockSpec((nb, N_HEADS, HEAD_DIM, HEAD_DIM), lambda i, j: (i, 0, 0, 0))
    kern = functools.partial(_rwkv_kernel, nb=nb, tb=tb, n_p=len(p_specs), sample=sample)
    return pl.pallas_call(
        kern,
        grid=(B // nb, T // tb),
        in_specs=p_specs + [fix_spec, st, row(RWKV_PROJ), row(BR_W), full(128, BR_W), row(BR_W), full(128, BR_W),
                            full(128, BR_W), row(BR_W), row(BR_W), row(BR_W), full(256, 256),
                            full(HEAD_DIM, BR_W), row(BR_W), row(BR_W)],
        out_specs=[ha_spec, st, pl.BlockSpec((nb, RWKV_PROJ), lambda i, j: (i, 0))],
        out_shape=[ha_shape, jax.ShapeDtypeStruct((B, N_HEADS, HEAD_DIM, HEAD_DIM), F32),
                   jax.ShapeDtypeStruct((B, RWKV_PROJ), F32)],
        scratch_shapes=[pltpu.VMEM((nb, HEAD_DIM, BR_W), F32), pltpu.VMEM((9, nb * tb, BR_W), F32),
                        pltpu.VMEM((nb * tb, BR_W), F32), pltpu.VMEM((nb, RWKV_PROJ), F32),
                        pltpu.VMEM((nb * tb * N_HEADS, HEAD_DIM), F32),
                        pltpu.VMEM((min(SCAN_SUB, tb), nb * HEAD_DIM, BR_W), F32)],
        compiler_params=_cparams(("parallel", "arbitrary")),
        name="rwkv",
    )(*([P] * len(p_specs)), fix, s0, lp['mu'], lp['w0'], lp['w2'], lp['a0'], lp['a2'], lp['g2'], lp['kk'],
      lp['ka'], lp['rk'], bo, _diag_mask(), lp['ln_g'], lp['ln_b'])


def _diag_mask():
    u = jnp.arange(HEAD_DIM)[:, None]
    j = jnp.arange(BR_W)[None, :] % HEAD_DIM
    return ((j == u) | (j == (u + HEAD_DIM // 2) % HEAD_DIM)).astype(F32)


def _pair_masks():
    lane = lax.broadcasted_iota(jnp.int32, (1, 128), 1)
    m0 = (lane < HEAD_DIM).astype(F32)
    return m0, 1.0 - m0


def _swap_halves(x):
    lane = lax.broadcasted_iota(jnp.int32, x.shape, 1)
    n = x.shape[1]
    return jnp.where((lane & 63) < 32, pltpu.roll(x, n - 32, axis=1), pltpu.roll(x, 32, axis=1))


def _ret_kernel(q_ref, k_ref, v_ref, z_ref, cos_ref, sin_ref, dm_ref, qin_ref, kend_ref, gs_ref, bd_ref, s0_ref,
                bo_ref, gn_ref, h_ref, sT_ref, S_scr, *, nseq, L):
    c = pl.program_id(1)

    @pl.when(c == 0)
    def _():
        _load_pairs(S_scr, s0_ref, nseq)

    cos = cos_ref[...]
    sin = sin_ref[...]
    q = q_ref[...]
    k = k_ref[...]
    q = q * cos + _swap_halves(q) * sin
    k = (k * cos + _swap_halves(k) * sin) * (HEAD_DIM ** -0.5)
    v = v_ref[...]
    qin = qin_ref[...]
    ke = k * kend_ref[...]
    m0, m1 = _pair_masks()
    rows = nseq * L
    rowi = lax.broadcasted_iota(jnp.int32, (rows, 1), 0)
    outs = []
    for p in range(N_HEADS // 2):
        sl = slice(128 * p, 128 * (p + 1))
        qp, kp, vp, kep = q[:, sl], k[:, sl], v[:, sl], ke[:, sl]
        kb = kp.astype(BF16)
        o = jnp.zeros((rows, 128), F32)
        for e, me in enumerate((m0, m1)):
            sc = _dot_nt((qp * me).astype(BF16), kb) * dm_ref[2 * p + e]
            o = o + _dot(sc.astype(BF16), (vp * me).astype(BF16))
        qb = qp.astype(BF16)
        for s in range(nseq):
            S = S_scr[s, p]
            oi = _dot(qb, S.astype(BF16)) * qin[:, sl]
            if nseq == 1:
                o = o + oi
                kes = kep
            else:
                inseq = (rowi >= s * L) & (rowi < (s + 1) * L)
                o = o + jnp.where(inseq, oi, 0.0)
                kes = jnp.where(inseq, kep, 0.0)
            S_scr[s, p] = S * gs_ref[p] + bd_ref[...] * _dot_tn(kes.astype(BF16), vp.astype(BF16))
        outs.append(o)
    o = jnp.concatenate(outs, axis=1)
    o = _head_norm(o, bo_ref[...], GN_EPS) * gn_ref[...]
    z = z_ref[...]
    h_ref[...] = z * _sigmoid(z) * o

    @pl.when(c == pl.num_programs(1) - 1)
    def _():
        _store_pairs(sT_ref, S_scr, nseq)


def _load_pairs(S_scr, s0_ref, nseq):
    S_scr[...] = jnp.zeros_like(S_scr)
    for s in range(nseq):
        for p in range(N_HEADS // 2):
            S_scr[s, p, 0:HEAD_DIM, 0:HEAD_DIM] = s0_ref[s, 2 * p]
            S_scr[s, p, HEAD_DIM:, HEAD_DIM:] = s0_ref[s, 2 * p + 1]


def _store_pairs(sT_ref, S_scr, nseq):
    for s in range(nseq):
        for p in range(N_HEADS // 2):
            sT_ref[s, 2 * p] = S_scr[s, p, 0:HEAD_DIM, 0:HEAD_DIM]
            sT_ref[s, 2 * p + 1] = S_scr[s, p, HEAD_DIM:, HEAD_DIM:]


def _retention(P, row0, n_groups, n_chunks, nseq, L, tabs, s0, bo, gn):
    rows = nseq * L
    rb0 = row0 // rows
    cos, sin, dm, qin, kend, gs, bd = tabs
    col = lambda cb: pl.BlockSpec((rows, BR_W), lambda i, c: (rb0 + i * n_chunks + c, cb))
    tab = pl.BlockSpec((rows, BR_W), lambda i, c: (c, 0))
    cst = lambda shp: pl.BlockSpec(shp, lambda i, c: (0,) * len(shp))
    st = pl.BlockSpec((nseq, N_HEADS, HEAD_DIM, HEAD_DIM), lambda i, c: (i, 0, 0, 0))
    kern = functools.partial(_ret_kernel, nseq=nseq, L=L)
    return pl.pallas_call(
        kern,
        grid=(n_groups, n_chunks),
        in_specs=[col(0), col(1), col(2), col(3), tab, tab, cst((N_HEADS, rows, rows)), cst((rows, BR_W)),
                  cst((rows, BR_W)), cst((4, 128, 128)), cst((128, 128)), st, cst((256, 256)), cst((1, BR_W))],
        out_specs=[pl.BlockSpec((rows, BR_W), lambda i, c: (i * n_chunks + c, 0)), st],
        out_shape=[jax.ShapeDtypeStruct((n_groups * n_chunks * rows, BR_W), F32),
                   jax.ShapeDtypeStruct((n_groups * nseq, N_HEADS, HEAD_DIM, HEAD_DIM), F32)],
        scratch_shapes=[pltpu.VMEM((nseq, 4, 128, 128), F32)],
        compiler_params=_cparams(("parallel", "arbitrary")),
        name="retention",
    )(P, P, P, P, cos, sin, dm, qin, kend, gs, bd, s0, bo, gn)


def _mlstm_kernel(qk_ref, v_ref, o_ref, if_ref, xcs_ref, cw_ref, cb_ref, bi_ref, bf_ref, tri_ref, bd_ref, eye_ref,
                  c0_ref, n0_ref, m0_ref, bo_ref, gn_ref, h_ref, cT_ref, nT_ref, mT_ref, convT_ref,
                  C_scr, n_scr, m_scr, xc_scr, *, nseq, L):
    c = pl.program_id(1)
    rows = nseq * L

    @pl.when(c == 0)
    def _():
        _load_pairs(C_scr, c0_ref, nseq)
        n_scr[...] = n0_ref[...]
        m_scr[...] = m0_ref[...]
        if nseq == 1:
            xc_scr[0:8, :] = jnp.zeros((8, 2 * BR_W), F32)

    cw = cw_ref[...]
    if nseq == 1:
        xc_scr[8:8 + L, :] = qk_ref[...]
        conv = cb_ref[...]
        for w in range(CONV_W):
            conv = conv + xc_scr[5 + w:5 + w + L, :] * cw[w:w + 1, :]
        xc_scr[0:8, :] = xc_scr[L:L + 8, :]
    else:
        parts = []
        for s in range(nseq):
            cs = cb_ref[...]
            for w in range(CONV_W):
                cs = cs + xcs_ref[s, 1 + w:1 + w + L, :] * cw[w:w + 1, :]
            parts.append(cs)
        conv = jnp.concatenate(parts, axis=0)
    qk = conv * _sigmoid(conv)
    q = qk[:, :BR_W]
    k = qk[:, BR_W:] * (HEAD_DIM ** -0.5)
    v = v_ref[...]

    rowi = lax.broadcasted_iota(jnp.int32, (rows, 1), 0)
    tok = rowi % L
    ig = if_ref[:, 0:128] + bi_ref[...]
    fp = if_ref[:, 128:256] + bf_ref[...]
    lf = jnp.minimum(fp, 0.0) - jnp.log(1.0 + jnp.exp(-jnp.abs(fp)))
    tri = tri_ref[...]
    l1, l2, l3 = _split3(lf)
    bcum = _dot(tri, l1) + _dot(tri, l2) + _dot(tri, l3)
    u = ig - bcum
    cm = u
    sh = 1
    while sh < L:
        cm = jnp.maximum(cm, jnp.where(tok >= sh, pltpu.roll(cm, sh, axis=0), -jnp.inf))
        sh *= 2
    if nseq == 1:
        m0r = m_scr[0]
    else:
        m0r = jnp.concatenate([jnp.broadcast_to(m_scr[s], (L, 128)) for s in range(nseq)], axis=0)
    mt = bcum + jnp.maximum(m0r, cm)
    inter = jnp.exp(bcum + m0r - mt)
    bmm = bcum - mt
    emt = jnp.exp(-mt)
    wend_parts = []
    for s in range(nseq):
        last = slice(s * L + L - 1, s * L + L)
        m_end = mt[last, :]
        wend_parts.append(jnp.exp(bcum[last, :] - m_end + u[s * L:(s + 1) * L, :]))
    wend = wend_parts[0] if nseq == 1 else jnp.concatenate(wend_parts, axis=0)
    eye = eye_ref[...]
    u1, u2, u3 = _split3(u)
    ut = _dot_nt(eye, u1) + _dot_nt(eye, u2) + _dot_nt(eye, u3)
    trib = tri > 0.5

    m0m, m1m = _pair_masks()
    lane128 = lax.broadcasted_iota(jnp.int32, (1, 128), 1)
    first = lane128 < HEAD_DIM
    bo = bo_ref[...]
    qn_all = None
    outs = []
    wk_all = []
    for p in range(N_HEADS // 2):
        sl = slice(128 * p, 128 * (p + 1))
        qp, kp, vp = q[:, sl], k[:, sl], v[:, sl]
        kb = kp.astype(BF16)
        num = jnp.zeros((rows, 128), F32)
        dsum = []
        for e, me in enumerate((m0m, m1m)):
            h = 2 * p + e
            logd = bmm[:, h:h + 1] + ut[h:h + 1, :]
            dmat = jnp.where(trib, jnp.exp(jnp.minimum(logd, 0.0)), 0.0)
            sc = _dot_nt((qp * me).astype(BF16), kb) * dmat
            num = num + _dot(sc.astype(BF16), (vp * me).astype(BF16))
            dsum.append(jnp.sum(sc, axis=1, keepdims=True))
        pick = lambda arr: jnp.where(first, arr[:, 2 * p:2 * p + 1], arr[:, 2 * p + 1:2 * p + 2])
        inter_p = pick(inter)
        den = jnp.where(first, dsum[0], dsum[1])
        wend_p = pick(wend)
        qb = qp.astype(BF16)
        kw = kp * wend_p
        wk_all.append(kw)
        qn = jnp.zeros((rows, 128), F32)
        for s in range(nseq):
            C = C_scr[s, p]
            nrow = n_scr[s][:, sl]
            qc = _dot(qb, C.astype(BF16))
            qns = qp * nrow
            if nseq == 1:
                num = num + inter_p * qc
                qn = qns
                kws = kw
                send = inter_p[L - 1:L, :]
            else:
                inseq = (rowi >= s * L) & (rowi < (s + 1) * L)
                num = num + jnp.where(inseq, inter_p * qc, 0.0)
                qn = qn + jnp.where(inseq, qns, 0.0)
                kws = jnp.where(inseq, kw, 0.0)
                send = inter_p[s * L + L - 1:s * L + L, :]
            send_col = jnp.where(lax.broadcasted_iota(jnp.int32, (128, 1), 0) < HEAD_DIM,
                                 send[:, 0:1], send[:, 64:65])
            C_scr[s, p] = C * send_col + bd_ref[...] * _dot_tn(kws.astype(BF16), vp.astype(BF16))
        den = den + inter_p * _segsum_pair(qn, bo)
        outs.append(num / jnp.maximum(jnp.abs(den), pick(emt)))
    hh = jnp.concatenate(outs, axis=1)
    kw_full = jnp.concatenate(wk_all, axis=1)
    for s in range(nseq):
        last = slice(s * L + L - 1, s * L + L)
        send_row = _expand_heads(inter[last, :])
        n_scr[s] = send_row * n_scr[s] + jnp.sum(kw_full[s * L:(s + 1) * L, :], axis=0, keepdims=True)
        m_scr[s] = mt[last, :]
    op = o_ref[...]
    hh = _sigmoid(op) * hh
    h_ref[...] = _head_norm(hh, bo, GN_EPS) * gn_ref[...]

    @pl.when(c == pl.num_programs(1) - 1)
    def _():
        _store_pairs(cT_ref, C_scr, nseq)
        nT_ref[...] = n_scr[...]
        mT_ref[...] = m_scr[...]
        if nseq == 1:
            convT_ref[0] = xc_scr[8 - (CONV_W - 1):8, :]
        else:
            for s in range(nseq):
                convT_ref[s] = xcs_ref[s, 8 - (CONV_W - 1):8, :]


def _segsum_pair(x, bo):
    hi, lo = _split2(x)
    b2 = bo[:128, :128]
    return _dot(hi, b2) + _dot(lo, b2)


def _expand_heads(row):
    lane = lax.broadcasted_iota(jnp.int32, (1, BR_W), 1)
    out = jnp.zeros((1, BR_W), F32)
    for h in range(N_HEADS):
        out = jnp.where((lane >> 6) == h, row[:, h:h + 1], out)
    return out


def _mlstm(P, row0, n_groups, n_chunks, nseq, L, xcs, lp, tabs, c0, n0, m0, bo):
    rows = nseq * L
    rb0 = row0 // rows
    tri, bd, eye = tabs
    rowblk = lambda width, cb: pl.BlockSpec((rows, width), lambda i, c: (rb0 + i * n_chunks + c, cb))
    cst = lambda shp: pl.BlockSpec(shp, lambda i, c: (0,) * len(shp))
    stC = pl.BlockSpec((nseq, N_HEADS, HEAD_DIM, HEAD_DIM), lambda i, c: (i, 0, 0, 0))
    stn = pl.BlockSpec((nseq, 1, BR_W), lambda i, c: (i, 0, 0))
    stm = pl.BlockSpec((nseq, 1, 128), lambda i, c: (i, 0, 0))
    if nseq == 1:
        xcs_spec = cst((1, 8, 2 * BR_W))
    else:
        xcs_spec = pl.BlockSpec((nseq, 8, 2 * BR_W), lambda i, c: (i, 0, 0))
    kern = functools.partial(_mlstm_kernel, nseq=nseq, L=L)
    nB = n_groups * nseq
    return pl.pallas_call(
        kern,
        grid=(n_groups, n_chunks),
        in_specs=[rowblk(2 * BR_W, COL_ML // (2 * BR_W)), rowblk(BR_W, (COL_ML + 1024) // BR_W),
                  rowblk(BR_W, (COL_ML + 1536) // BR_W), rowblk(256, COL_IF // 256), xcs_spec,
                  cst((CONV_W, 2 * BR_W)), cst((1, 2 * BR_W)), cst((1, 128)), cst((1, 128)),
                  cst((rows, rows)), cst((128, 128)), cst((128, 128)), stC, stn, stm, cst((256, 256)),
                  cst((1, BR_W))],
        out_specs=[pl.BlockSpec((rows, BR_W), lambda i, c: (i * n_chunks + c, 0)), stC, stn, stm,
                   pl.BlockSpec((nseq, CONV_W - 1, 2 * BR_W), lambda i, c: (i, 0, 0))],
        out_shape=[jax.ShapeDtypeStruct((n_groups * n_chunks * rows, BR_W), F32),
                   jax.ShapeDtypeStruct((nB, N_HEADS, HEAD_DIM, HEAD_DIM), F32),
                   jax.ShapeDtypeStruct((nB, 1, BR_W), F32),
                   jax.ShapeDtypeStruct((nB, 1, 128), F32),
                   jax.ShapeDtypeStruct((nB, CONV_W - 1, 2 * BR_W), F32)],
        scratch_shapes=[pltpu.VMEM((nseq, 4, 128, 128), F32), pltpu.VMEM((nseq, 1, BR_W), F32),
                        pltpu.VMEM((nseq, 1, 128), F32), pltpu.VMEM((L + 8, 2 * BR_W), F32)],
        compiler_params=_cparams(("parallel", "arbitrary")),
        name="mlstm",
    )(P, P, P, P, xcs, lp['conv_w'], lp['conv_b'], lp['b_i'], lp['b_f'], tri, bd, eye, c0, n0, m0, bo,
      lp['mlstm_gn'])


def _mix_kernel(x_ref, hap_ref, has_ref, hrp_ref, hrs_ref, hmp_ref, hms_ref, g0_ref, g1_ref, g2_ref, wb_ref,
                bg_ref, wo_ref, o_ref, *, n_p):
    def body(h_refs):
        mixed = None
        for n, (h_ref, pg_ref) in enumerate(zip(h_refs, (g0_ref, g1_ref, g2_ref))):
            ub = _dot(h_ref[...].astype(BF16), wb_ref[n])
            t = _sigmoid(pg_ref[...] + bg_ref[n]) * ub
            mixed = t if mixed is None else mixed + t
        o_ref[...] = x_ref[...] + _dot(mixed.astype(BF16), wo_ref[...])

    @pl.when(pl.program_id(0) < n_p)
    def _():
        body((hap_ref, hrp_ref, hmp_ref))

    @pl.when(pl.program_id(0) >= n_p)
    def _():
        body((has_ref, hrs_ref, hms_ref))


def _mix(x, hp, hs, P, wb, bg, wo, tm):
    R = x.shape[0]
    n_p = hp[0].shape[0] // tm
    xs = pl.BlockSpec((tm, D_MODEL), lambda i: (i, 0))
    hps = pl.BlockSpec((tm, BR_W), lambda i: (jnp.minimum(i, n_p - 1), 0))
    hss = pl.BlockSpec((tm, BR_W), lambda i: (jnp.maximum(i - n_p, 0), 0))
    gs = lambda n: pl.BlockSpec((tm, D_MODEL), lambda i: (i, COL_GATE // D_MODEL + n))
    return pl.pallas_call(
        functools.partial(_mix_kernel, n_p=n_p),
        grid=(R // tm,),
        in_specs=[xs, hps, hss, hps, hss, hps, hss, gs(0), gs(1), gs(2),
                  pl.BlockSpec((3, BR_W, D_MODEL), lambda i: (0, 0, 0)),
                  pl.BlockSpec((3, 1, D_MODEL), lambda i: (0, 0, 0)),
                  pl.BlockSpec((D_MODEL, D_MODEL), lambda i: (0, 0))],
        out_specs=xs,
        out_shape=jax.ShapeDtypeStruct((R, D_MODEL), F32),
        compiler_params=_cparams(("parallel",)),
        name="mix_out",
    )(x, hp[0], hs[0], hp[1], hs[1], hp[2], hs[2], P, P, P, wb, bg, wo)


def _ffn_kernel(x_ref, g_ref, w1_ref, w2_ref, gf_ref, o_ref, xn_scr, acc_scr, *, final):
    j = pl.program_id(1)

    @pl.when(j == 0)
    def _():
        xn_scr[...] = _rms(x_ref[...], g_ref[...]).astype(BF16)
        acc_scr[...] = jnp.zeros_like(acc_scr)

    h = jnp.maximum(_dot(xn_scr[...], w1_ref[...]), 0.0)
    acc_scr[...] += _dot((h * h).astype(BF16), w2_ref[...])

    @pl.when(j == pl.num_programs(1) - 1)
    def _():
        y = x_ref[...] + acc_scr[...]
        if final:
            y = _rms(y, gf_ref[...])
        o_ref[...] = y


def _ffn(x, g, w1, w2, gf, tm, tf, final):
    R = x.shape[0]
    kern = functools.partial(_ffn_kernel, final=final)
    return pl.pallas_call(
        kern,
        grid=(R // tm, D_FF // tf),
        in_specs=[pl.BlockSpec((tm, D_MODEL), lambda i, j: (i, 0)),
                  pl.BlockSpec((1, D_MODEL), lambda i, j: (0, 0)),
                  pl.BlockSpec((D_MODEL, tf), lambda i, j: (0, j)),
                  pl.BlockSpec((tf, D_MODEL), lambda i, j: (j, 0)),
                  pl.BlockSpec((1, D_MODEL), lambda i, j: (0, 0))],
        out_specs=pl.BlockSpec((tm, D_MODEL), lambda i, j: (i, 0)),
        out_shape=jax.ShapeDtypeStruct((R, D_MODEL), F32),
        scratch_shapes=[pltpu.VMEM((tm, D_MODEL), BF16), pltpu.VMEM((tm, D_MODEL), F32)],
        compiler_params=_cparams(("parallel", "arbitrary")),
        name="ffn",
    )(x, g, w1, w2, gf)


def _rope_tables(T, pos0):
    half = HEAD_DIM // 2
    inv = jnp.power(ROPE_BASE, -jnp.arange(half, dtype=F32) / half)
    ang = (jnp.arange(T, dtype=F32) + pos0)[:, None] * inv[None, :]
    cos, sin = jnp.cos(ang), jnp.sin(ang)
    cos_h = jnp.concatenate([cos, cos], axis=1)
    sin_h = jnp.concatenate([-sin, sin], axis=1)
    return jnp.tile(cos_h, (1, N_HEADS)), jnp.tile(sin_h, (1, N_HEADS))


def _ret_tables(nseq, L):
    log_g = jnp.log(1.0 - jnp.exp2(-5.0 - jnp.arange(N_HEADS, dtype=F32)))
    idx = jnp.arange(L, dtype=F32)
    diff = idx[:, None] - idx[None, :]
    dmask = jnp.where(diff[None] >= 0, jnp.exp(jnp.maximum(diff, 0.0)[None] * log_g[:, None, None]), 0.0)
    q_in = jnp.exp((idx + 1.0)[:, None] * log_g[None, :])
    k_end = jnp.exp((L - 1.0 - idx)[:, None] * log_g[None, :])
    g_chunk = jnp.exp(L * log_g)
    eye = jnp.eye(nseq, dtype=F32)
    dm = jnp.einsum('st,hij->hsitj', eye, dmask).reshape(N_HEADS, nseq * L, nseq * L)
    qin = jnp.tile(jnp.repeat(q_in, HEAD_DIM, axis=1), (nseq, 1))
    kend = jnp.tile(jnp.repeat(k_end, HEAD_DIM, axis=1), (nseq, 1))
    gs = jnp.broadcast_to(jnp.repeat(g_chunk, HEAD_DIM).reshape(4, 128, 1), (4, 128, 128))
    return dm, qin, kend, gs


def _block_diag_ones(n):
    i = jnp.arange(n) // HEAD_DIM
    return (i[:, None] == i[None, :])


def _seq_tri(nseq, L):
    r = jnp.arange(nseq * L)
    return ((r[:, None] // L == r[None, :] // L) & (r[:, None] >= r[None, :])).astype(BF16)


def _pick_tile(n, cands):
    for c in cands:
        if n % c == 0:
            return c
    raise ValueError(f"no tile for {n}")


def kernel(x_prompt, x_sample, state_rwkv_wkv, state_rwkv_shift, state_ret, state_mlstm_C, state_mlstm_n,
           state_mlstm_m, state_mlstm_conv, norm_mix, w_in, rwkv_mu, rwkv_w0, rwkv_w2, rwkv_a0, rwkv_a2,
           rwkv_g2, rwkv_kk, rwkv_ka, rwkv_rk, rwkv_ln_g, rwkv_ln_b, ret_gn_g, mlstm_conv_w, mlstm_conv_b,
           mlstm_b_i, mlstm_b_f, mlstm_gn_g, w_branch, b_gate, w_out, norm_ffn, w_ff1, w_ff2, norm_final):
    Bp, Tp, _ = x_prompt.shape
    Bs, Ts, _ = x_sample.shape
    Rp, Rs = Bp * Tp, Bs * Ts
    R = Rp + Rs
    past_len = 16384
    Lp = min(PROMPT_CHUNK, Tp)
    assert Tp % Lp == 0 and Bs % SAMPLE_SEQS == 0 and Rp % (SAMPLE_SEQS * Ts) == 0
    assert Tp & (Tp - 1) == 0 and Ts & (Ts - 1) == 0 and Tp >= CONV_W - 1 and Bp % SCAN_BATCH == 0

    x = jnp.concatenate([x_prompt.reshape(Rp, D_MODEL), x_sample.reshape(Rs, D_MODEL)], axis=0)

    o_rw, o_ret, o_ml = 0, RWKV_PROJ, RWKV_PROJ + 4 * BR_W
    o_if = o_ml + 4 * BR_W
    o_gate = o_if + 2 * N_HEADS
    zpad = lambda n: jnp.zeros((DEPTH, D_MODEL, n), BF16)
    w_inb = w_in.astype(BF16)
    w_cat = jnp.concatenate([
        w_inb[:, :, o_ret:o_ret + 4 * BR_W], w_inb[:, :, o_ml:o_ml + 4 * BR_W], w_inb[:, :, o_rw:o_rw + RWKV_PROJ],
        w_inb[:, :, o_if:o_if + N_HEADS], zpad(128 - N_HEADS), w_inb[:, :, o_if + N_HEADS:o_gate],
        zpad(128 - N_HEADS), w_inb[:, :, o_gate:]], axis=2)
    pad_rows = lambda a, top: jnp.concatenate(
        [jnp.zeros((DEPTH, top, BR_W), F32), a, jnp.zeros((DEPTH, 128 - top - a.shape[1], BR_W), F32)], axis=1)
    w2p = pad_rows(rwkv_w2, 0).astype(BF16)
    a2p = pad_rows(rwkv_a2, W_LORA).astype(BF16)
    g2b = rwkv_g2.astype(BF16)
    pad128 = lambda a: jnp.concatenate([a, jnp.zeros((DEPTH, 128 - N_HEADS), F32)], axis=1)
    wbb, wob, w1b, w2b = (t.astype(BF16) for t in (w_branch, w_out, w_ff1, w_ff2))
    bo = _block_diag_ones(256).astype(BF16)
    bd = _block_diag_ones(128).astype(F32)
    eye = jnp.eye(128, dtype=BF16)

    cos_p, sin_p = _rope_tables(Tp, 0)
    cos_s, sin_s = _rope_tables(Ts, past_len)
    cos_s, sin_s = jnp.tile(cos_s, (SAMPLE_SEQS, 1)), jnp.tile(sin_s, (SAMPLE_SEQS, 1))
    rt_p = _ret_tables(1, Lp)
    rt_s = _ret_tables(SAMPLE_SEQS, Ts)
    tri_p, tri_s = _seq_tri(1, Lp), _seq_tri(SAMPLE_SEQS, Ts)

    tm_in = _pick_tile(R, (1536, 768, 576, 192, 64))
    tm_mix = _pick_tile(math.gcd(Rp, Rs), (512, 64))
    tb_p = min(SCAN_TBLK, Tp)

    zeros_heads = jnp.zeros((Bp, N_HEADS, HEAD_DIM, HEAD_DIM), F32)
    outs = [[] for _ in range(14)]
    for l in range(DEPTH):
        lp = dict(mu=rwkv_mu[l][None], w0=rwkv_w0[l][None], w2=w2p[l], a0=rwkv_a0[l][None], a2=a2p[l], g2=g2b[l],
                  kk=rwkv_kk[l][None], ka=rwkv_ka[l][None], rk=rwkv_rk[l].reshape(1, BR_W),
                  ln_g=rwkv_ln_g[l][None], ln_b=rwkv_ln_b[l][None],
                  conv_w=mlstm_conv_w[l], conv_b=mlstm_conv_b[l][None], b_i=pad128(mlstm_b_i)[l][None],
                  b_f=pad128(mlstm_b_f)[l][None], mlstm_gn=mlstm_gn_g[l][None])
        P = _inproj(x, norm_mix[l][None], w_cat[l], tm_in, 1536)

        fix_s = jnp.repeat(state_rwkv_shift[l], Ts, axis=0)
        ha_p, wkv_p, shift_p = _rwkv(P, 0, Bp, Tp, tb_p, jnp.zeros((8, 128), F32), zeros_heads, lp, bo, False)
        ha_s, wkv_s, shift_s = _rwkv(P, Rp, Bs, Ts, Ts, fix_s, state_rwkv_wkv[l], lp, bo, True)
        ha_p = ha_p.reshape(Rp, BR_W)

        gn = ret_gn_g[l][None]
        hr_p, ret_p = _retention(P, 0, Bp, Tp // Lp, 1, Lp, (cos_p, sin_p) + rt_p + (bd,), zeros_heads, bo, gn)
        hr_s, ret_s = _retention(P, Rp, Bs // SAMPLE_SEQS, 1, SAMPLE_SEQS, Ts, (cos_s, sin_s) + rt_s + (bd,),
                                 state_ret[l], bo, gn)

        qk_s = lax.slice(P, (Rp, COL_ML), (R, COL_ML + 2 * BR_W)).reshape(Bs, Ts, 2 * BR_W)
        xcs = jnp.concatenate([jnp.zeros((Bs, 1, 2 * BR_W), F32), state_mlstm_conv[l], qk_s], axis=1)
        hm_p, c_p, n_p, m_p, conv_p = _mlstm(P, 0, Bp, Tp // Lp, 1, Lp, jnp.zeros((1, 8, 2 * BR_W), F32), lp,
                                             (tri_p, bd, eye), zeros_heads, jnp.zeros((Bp, 1, BR_W), F32),
                                             jnp.zeros((Bp, 1, 128), F32), bo)
        m0_s = jnp.concatenate([state_mlstm_m[l], jnp.zeros((Bs, 128 - N_HEADS), F32)], axis=1)[:, None]
        hm_s, c_s, n_s, m_s, conv_s = _mlstm(P, Rp, Bs // SAMPLE_SEQS, 1, SAMPLE_SEQS, Ts, xcs, lp,
                                             (tri_s, bd, eye), state_mlstm_C[l],
                                             state_mlstm_n[l].reshape(Bs, 1, BR_W), m0_s, bo)

        x = _mix(x, (ha_p, hr_p, hm_p), (ha_s, hr_s, hm_s), P, wbb[l], b_gate[l][:, None], wob[l], tm_mix)
        x = _ffn(x, norm_ffn[l][None], w1b[l], w2b[l], norm_final[None], tm_in, 512, l == DEPTH - 1)

        for i, t in enumerate((wkv_p, shift_p, ret_p, c_p, n_p.reshape(Bp, N_HEADS, HEAD_DIM),
                               m_p[:, 0, :N_HEADS], conv_p,
                               wkv_s, shift_s, ret_s, c_s, n_s.reshape(Bs, N_HEADS, HEAD_DIM),
                               m_s[:, 0, :N_HEADS], conv_s)):
            outs[i].append(t)

    y_p = x[:Rp].reshape(Bp, Tp, D_MODEL)
    y_s = x[Rp:].reshape(Bs, Ts, D_MODEL)
    return (y_p, y_s) + tuple(jnp.stack(o, axis=0) for o in outs)
```

```python
import functools
import math

import jax
import jax.numpy as jnp
from jax import lax
from jax.experimental import pallas as pl
from jax.experimental.pallas import tpu as pltpu

F32 = jnp.float32
BF16 = jnp.bfloat16

D_MODEL = 1024
DEPTH = 4
HEAD_DIM = 64
N_HEADS = 8
BR_W = N_HEADS * HEAD_DIM
W_LORA = 64
A_LORA = 64
G_LORA = 128
CONV_W = 4
D_FF = 4 * D_MODEL
ROPE_BASE = 10000.0
RMS_EPS = 1e-6
RWKV_GN_EPS = 64e-5
GN_EPS = 1e-5
RWKV_PROJ = 3 * BR_W + W_LORA + A_LORA + G_LORA

COL_RET = 0
COL_ML = 2048
COL_RW = 4096
COL_IF = COL_RW + RWKV_PROJ
COL_GATE = 6144
N_PROJ = COL_GATE + 3 * D_MODEL

PROMPT_CHUNK = 128
SAMPLE_SEQS = 8
SCAN_BATCH = 8
SCAN_TBLK = 64
SCAN_SUB = 8
VMEM_LIMIT = 56 * 1024 * 1024


def _cparams(sem):
    return pltpu.CompilerParams(dimension_semantics=sem, vmem_limit_bytes=VMEM_LIMIT)


def _dot(a, b):
    return jnp.dot(a, b, preferred_element_type=F32)


def _dot_nt(a, b):
    return lax.dot_general(a, b, (((1,), (1,)), ((), ())), preferred_element_type=F32)


def _dot_tn(a, b):
    return lax.dot_general(a, b, (((0,), (0,)), ((), ())), preferred_element_type=F32)


def _split2(x):
    hi = x.astype(BF16)
    lo = (x - hi.astype(F32)).astype(BF16)
    return hi, lo


def _split3(x):
    x1 = x.astype(BF16)
    r1 = x - x1.astype(F32)
    x2 = r1.astype(BF16)
    x3 = (r1 - x2.astype(F32)).astype(BF16)
    return x1, x2, x3


def _sigmoid(x):
    return 1.0 / (1.0 + jnp.exp(-x))


def _softplus(x):
    return jnp.maximum(x, 0.0) + jnp.log(1.0 + jnp.exp(-jnp.abs(x)))


def _seg_bf16(xb, bo):
    return jnp.concatenate([_dot(xb[:, :256], bo), _dot(xb[:, 256:], bo)], axis=1)


def _segsum(x, bo):
    hi, lo = _split2(x)
    return _seg_bf16(hi, bo) + _seg_bf16(lo, bo)


def _head_norm(h, bo, eps):
    mu = _segsum(h, bo) * (1.0 / HEAD_DIM)
    d = h - mu
    var = _segsum(d * d, bo) * (1.0 / HEAD_DIM)
    return d * lax.rsqrt(var + eps)


def _rms(x, g):
    return x * lax.rsqrt(jnp.mean(x * x, axis=-1, keepdims=True) + RMS_EPS) * g


def _inproj_kernel(x_ref, g_ref, w_ref, o_ref, xn_scr):
    @pl.when(pl.program_id(1) == 0)
    def _():
        xn_scr[...] = _rms(x_ref[...], g_ref[...]).astype(BF16)

    o_ref[...] = _dot(xn_scr[...], w_ref[...])


def _inproj(x, g, w, tm, tn):
    R = x.shape[0]
    return pl.pallas_call(
        _inproj_kernel,
        grid=(R // tm, N_PROJ // tn),
        in_specs=[pl.BlockSpec((tm, D_MODEL), lambda i, j: (i, 0)),
                  pl.BlockSpec((1, D_MODEL), lambda i, j: (0, 0)),
                  pl.BlockSpec((D_MODEL, tn), lambda i, j: (0, j))],
        out_specs=pl.BlockSpec((tm, tn), lambda i, j: (i, j)),
        out_shape=jax.ShapeDtypeStruct((R, N_PROJ), F32),
        scratch_shapes=[pltpu.VMEM((tm, D_MODEL), BF16)],
        compiler_params=_cparams(("parallel", "arbitrary")),
        name="inproj",
    )(x, g, w)


def _rwkv_kernel(*refs, nb, tb, n_p, sample):
    p_refs = refs[:n_p]
    (fix_ref, s0_ref, mu_ref, w0_ref, w2_ref, a0_ref, a2_ref, g2_ref, kkp_ref, kap_ref, rk_ref, bo_ref, mk_ref,
     lng_ref, lnb_ref, _, ha_ref, sT_ref, shift_ref, S_scr, seq_scr, o_scr, carry_scr, o8_scr,
     vc_scr) = refs[n_p:]
    sub = min(SCAN_SUB, tb)
    j = pl.program_id(1)
    rows = nb * tb

    @pl.when(j == 0)
    def _():
        for b in range(nb):
            for h in range(N_HEADS):
                S_scr[b, :, h * HEAD_DIM:(h + 1) * HEAD_DIM] = s0_ref[b, h]
        carry_scr[...] = jnp.zeros_like(carry_scr)

    if sample:
        x = p_refs[0][:, :RWKV_PROJ]
        tok = lax.broadcasted_iota(jnp.int32, (rows, 1), 0) & (tb - 1)
        prev = jnp.where(tok == 0, fix_ref[...], pltpu.roll(x, 1, axis=0))
    else:
        first = lax.broadcasted_iota(jnp.int32, (tb, 1), 0) == 0
        xs = [p_refs[b][:, :RWKV_PROJ] for b in range(nb)]
        prev = jnp.concatenate([jnp.where(first, carry_scr[b:b + 1, :], pltpu.roll(xs[b], 1, axis=0))
                                for b in range(nb)], axis=0)
        for b in range(nb):
            carry_scr[b:b + 1, :] = xs[b][tb - 1:tb, :]
        x = jnp.concatenate(xs, axis=0)
    pm = x + (prev - x) * mu_ref[...]
    r = pm[:, 0:512]
    k = pm[:, 512:1024]
    v = pm[:, 1024:1536]
    wa = pm[:, 1536:1664]
    gd = pm[:, 1664:1792]
    bo = bo_ref[...]
    lw = _dot(jnp.tanh(wa).astype(BF16), w2_ref[...])
    w_log = -_softplus(-(w0_ref[...] + lw)) - 0.5
    a = _sigmoid(a0_ref[...] + _dot(wa.astype(BF16), a2_ref[...]))
    kk = k * kkp_ref[...]
    kk = kk / jnp.maximum(jnp.sqrt(_segsum(kk * kk, bo)), 1e-12)
    kt = k * (1.0 + (a - 1.0) * kap_ref[...])
    seq_scr[0] = r
    seq_scr[1] = jnp.exp(-jnp.exp(w_log))
    seq_scr[2] = kt
    seq_scr[3] = -(kk * a)
    seq_scr[4] = kk
    seq_scr[6] = _dot(_sigmoid(gd).astype(BF16), g2_ref[...])
    seq_scr[7] = _segsum(r * kt * rk_ref[...], bo) * v
    v_hi = v.astype(BF16).astype(F32)
    v_lo = _swap_halves(v - v_hi)
    low_half = (lax.broadcasted_iota(jnp.int32, (1, BR_W), 1) & (HEAD_DIM - 1)) < HEAD_DIM // 2
    seq_scr[5] = jnp.where(low_half, v_hi, v_lo)
    seq_scr[8] = jnp.where(low_half, v_lo, v_hi)

    blk = lambda arr, b: arr[b * HEAD_DIM:(b + 1) * HEAD_DIM]
    half = HEAD_DIM // 2
    head_mask = (lax.broadcasted_iota(jnp.int32, (N_HEADS, BR_W), 1) // HEAD_DIM
                 == lax.broadcasted_iota(jnp.int32, (N_HEADS, BR_W), 0)).astype(F32)

    def emit_out(t, s_bf):
        for b in range(nb):
            r8 = (seq_scr[0, pl.ds(b * tb + t, 1), :] * head_mask).astype(BF16)
            o8 = _dot_nt(r8, s_bf[b])
            o8_scr[pl.ds(pl.multiple_of((b * tb + t) * N_HEADS, N_HEADS), N_HEADS), :] = o8

    def step(t, tt):
        row = lambda q, b: seq_scr[q, pl.ds(b * tb + t, 1), :]
        s_bf = [S_scr[b].astype(BF16) for b in range(nb)]
        emit_out(jnp.maximum(t - 1, 0), s_bf)
        sk = _seg_bf16(jnp.concatenate([s_bf[b] * row(4, b).astype(BF16) for b in range(nb)], axis=0), bo)
        for b in range(nb):
            vc = vc_scr[tt, b * HEAD_DIM:(b + 1) * HEAD_DIM, :]
            S_scr[b] = S_scr[b] * row(1, b) + blk(sk, b) * row(3, b) + vc * row(2, b)

    def sub_block(sb, carry):
        t0 = sb * sub
        lhs = []
        for tt in range(sub):
            for b in range(nb):
                lhs.append((seq_scr[5, pl.ds(b * tb + t0 + tt, 1), :] * mk_ref[:half]).astype(BF16))
                lhs.append((seq_scr[8, pl.ds(b * tb + t0 + tt, 1), :] * mk_ref[half:]).astype(BF16))
        vc_all = _seg_bf16(jnp.concatenate(lhs, axis=0), bo)
        for tt in range(sub):
            vc_scr[tt] = vc_all[tt * nb * HEAD_DIM:(tt + 1) * nb * HEAD_DIM]
        for tt in range(sub):
            step(t0 + tt, tt)
        return carry

    lax.fori_loop(0, tb // sub, sub_block, 0)
    emit_out(tb - 1, [S_scr[b].astype(BF16) for b in range(nb)])

    for h in range(N_HEADS):
        o_scr[:, h * HEAD_DIM:(h + 1) * HEAD_DIM] = o8_scr[pl.ds(h, rows, stride=N_HEADS), :]

    o = _head_norm(o_scr[...], bo, RWKV_GN_EPS) * lng_ref[...] + lnb_ref[...]
    ha = (o + seq_scr[7]) * seq_scr[6]
    if sample:
        ha_ref[...] = ha
    else:
        for b in range(nb):
            ha_ref[b] = ha[b * tb:(b + 1) * tb]

    @pl.when(j == pl.num_programs(1) - 1)
    def _():
        for b in range(nb):
            for h in range(N_HEADS):
                sT_ref[b, h] = S_scr[b, :, h * HEAD_DIM:(h + 1) * HEAD_DIM]
            shift_ref[b:b + 1, :] = x[b * tb + tb - 1:b * tb + tb, :]


def _rwkv(P, row0, B, T, tb, fix, s0, l_in, acc, l_out, lp, bo, sample):
    nb = SCAN_BATCH
    cb = COL_RW // 2048
    if sample:
        assert tb == T
        p_specs = [pl.BlockSpec((nb * T, 2048), lambda i, j: (row0 // (nb * T) + i, cb))]
        fix_spec = pl.BlockSpec((nb * T, RWKV_PROJ), lambda i, j: (i, 0))
        ha_spec = pl.BlockSpec((nb * T, BR_W), lambda i, j: (i, 0))
        ha_shape = jax.ShapeDtypeStruct((B * T, BR_W), F32)
    else:
        p_specs = [pl.BlockSpec((tb, 2048), lambda i, j, b=b: (row0 // tb + (i * nb + b) * (T // tb) + j, cb))
                   for b in range(nb)]
        fix_spec = pl.BlockSpec(fix.shape, lambda i, j: (0, 0))
        ha_spec = pl.BlockSpec((nb, tb, BR_W), lambda i, j: (i, j, 0))
        ha_shape = jax.ShapeDtypeStruct((B, T, BR_W), F32)
    row = lambda n: pl.BlockSpec((1, n), lambda i, j: (0, 0))
    full = lambda a, b: pl.BlockSpec((a, b), lambda i, j: (0, 0))
    st_in, st_out = _state_specs(nb, l_in, l_out)
    kern = functools.partial(_rwkv_kernel, nb=nb, tb=tb, n_p=len(p_specs), sample=sample)
    in_specs = p_specs + [fix_spec, st_in, row(RWKV_PROJ), row(BR_W), full(128, BR_W), row(BR_W), full(128, BR_W),
                          full(128, BR_W), row(BR_W), row(BR_W), row(BR_W), full(256, 256),
                          full(HEAD_DIM, BR_W), row(BR_W), row(BR_W), pl.BlockSpec(memory_space=pl.ANY)]
    return pl.pallas_call(
        kern,
        grid=(B // nb, T // tb),
        in_specs=in_specs,
        out_specs=[ha_spec, st_out, pl.BlockSpec((nb, RWKV_PROJ), lambda i, j: (i, 0))],
        out_shape=[ha_shape, jax.ShapeDtypeStruct(acc.shape, F32), jax.ShapeDtypeStruct((B, RWKV_PROJ), F32)],
        input_output_aliases={len(in_specs) - 1: 1},
        scratch_shapes=[pltpu.VMEM((nb, HEAD_DIM, BR_W), F32), pltpu.VMEM((9, nb * tb, BR_W), F32),
                        pltpu.VMEM((nb * tb, BR_W), F32), pltpu.VMEM((nb, RWKV_PROJ), F32),
                        pltpu.VMEM((nb * tb * N_HEADS, HEAD_DIM), F32),
                        pltpu.VMEM((min(SCAN_SUB, tb), nb * HEAD_DIM, BR_W), F32)],
        compiler_params=_cparams(("parallel", "arbitrary")),
        name="rwkv",
    )(*([P] * len(p_specs)), fix, s0, lp['mu'], lp['w0'], lp['w2'], lp['a0'], lp['a2'], lp['g2'], lp['kk'],
      lp['ka'], lp['rk'], bo, _diag_mask(), lp['ln_g'], lp['ln_b'], acc)


def _state_specs(nseq, l_in, l_out):
    mk = lambda l: pl.BlockSpec((None, nseq, N_HEADS, HEAD_DIM, HEAD_DIM), lambda i, c: (l, i, 0, 0, 0))
    return mk(l_in), mk(l_out)


def _diag_mask():
    u = jnp.arange(HEAD_DIM)[:, None]
    j = jnp.arange(BR_W)[None, :] % HEAD_DIM
    return ((j == u) | (j == (u + HEAD_DIM // 2) % HEAD_DIM)).astype(F32)


def _pair_masks():
    lane = lax.broadcasted_iota(jnp.int32, (1, 128), 1)
    m0 = (lane < HEAD_DIM).astype(F32)
    return m0, 1.0 - m0


def _swap_halves(x):
    lane = lax.broadcasted_iota(jnp.int32, x.shape, 1)
    n = x.shape[1]
    return jnp.where((lane & 63) < 32, pltpu.roll(x, n - 32, axis=1), pltpu.roll(x, 32, axis=1))


def _ret_kernel(q_ref, k_ref, v_ref, z_ref, cos_ref, sin_ref, dm_ref, qin_ref, kend_ref, gs_ref, bd_ref, s0_ref,
                bo_ref, gn_ref, _, h_ref, sT_ref, S_scr, *, nseq, L):
    c = pl.program_id(1)

    @pl.when(c == 0)
    def _():
        _load_pairs(S_scr, s0_ref, nseq)

    cos = cos_ref[...]
    sin = sin_ref[...]
    q = q_ref[...]
    k = k_ref[...]
    q = q * cos + _swap_halves(q) * sin
    k = (k * cos + _swap_halves(k) * sin) * (HEAD_DIM ** -0.5)
    v = v_ref[...]
    qin = qin_ref[...]
    ke = k * kend_ref[...]
    m0, m1 = _pair_masks()
    rows = nseq * L
    rowi = lax.broadcasted_iota(jnp.int32, (rows, 1), 0)
    outs = []
    for p in range(N_HEADS // 2):
        sl = slice(128 * p, 128 * (p + 1))
        qp, kp, vp, kep = q[:, sl], k[:, sl], v[:, sl], ke[:, sl]
        kb = kp.astype(BF16)
        o = jnp.zeros((rows, 128), F32)
        for e, me in enumerate((m0, m1)):
            sc = _dot_nt((qp * me).astype(BF16), kb) * dm_ref[2 * p + e]
            o = o + _dot(sc.astype(BF16), (vp * me).astype(BF16))
        qb = qp.astype(BF16)
        for s in range(nseq):
            S = S_scr[s, p]
            oi = _dot(qb, S.astype(BF16)) * qin[:, sl]
            if nseq == 1:
                o = o + oi
                kes = kep
            else:
                inseq = (rowi >= s * L) & (rowi < (s + 1) * L)
                o = o + jnp.where(inseq, oi, 0.0)
                kes = jnp.where(inseq, kep, 0.0)
            S_scr[s, p] = S * gs_ref[p] + bd_ref[...] * _dot_tn(kes.astype(BF16), vp.astype(BF16))
        outs.append(o)
    o = jnp.concatenate(outs, axis=1)
    o = _head_norm(o, bo_ref[...], GN_EPS) * gn_ref[...]
    z = z_ref[...]
    h_ref[...] = z * _sigmoid(z) * o

    @pl.when(c == pl.num_programs(1) - 1)
    def _():
        _store_pairs(sT_ref, S_scr, nseq)


def _load_pairs(S_scr, s0_ref, nseq):
    S_scr[...] = jnp.zeros_like(S_scr)
    for s in range(nseq):
        for p in range(N_HEADS // 2):
            S_scr[s, p, 0:HEAD_DIM, 0:HEAD_DIM] = s0_ref[s, 2 * p]
            S_scr[s, p, HEAD_DIM:, HEAD_DIM:] = s0_ref[s, 2 * p + 1]


def _store_pairs(sT_ref, S_scr, nseq):
    for s in range(nseq):
        for p in range(N_HEADS // 2):
            sT_ref[s, 2 * p] = S_scr[s, p, 0:HEAD_DIM, 0:HEAD_DIM]
            sT_ref[s, 2 * p + 1] = S_scr[s, p, HEAD_DIM:, HEAD_DIM:]


def _retention(P, row0, n_groups, n_chunks, nseq, L, tabs, s0, l_in, acc, l_out, bo, gn):
    rows = nseq * L
    rb0 = row0 // rows
    cos, sin, dm, qin, kend, gs, bd = tabs
    col = lambda cb: pl.BlockSpec((rows, BR_W), lambda i, c: (rb0 + i * n_chunks + c, cb))
    tab = pl.BlockSpec((rows, BR_W), lambda i, c: (c, 0))
    cst = lambda shp: pl.BlockSpec(shp, lambda i, c: (0,) * len(shp))
    st_in, st_out = _state_specs(nseq, l_in, l_out)
    kern = functools.partial(_ret_kernel, nseq=nseq, L=L)
    in_specs = [col(0), col(1), col(2), col(3), tab, tab, cst((N_HEADS, rows, rows)), cst((rows, BR_W)),
                cst((rows, BR_W)), cst((4, 128, 128)), cst((128, 128)), st_in, cst((256, 256)), cst((1, BR_W)),
                pl.BlockSpec(memory_space=pl.ANY)]
    return pl.pallas_call(
        kern,
        grid=(n_groups, n_chunks),
        in_specs=in_specs,
        out_specs=[pl.BlockSpec((rows, BR_W), lambda i, c: (i * n_chunks + c, 0)), st_out],
        out_shape=[jax.ShapeDtypeStruct((n_groups * n_chunks * rows, BR_W), F32),
                   jax.ShapeDtypeStruct(acc.shape, F32)],
        input_output_aliases={len(in_specs) - 1: 1},
        scratch_shapes=[pltpu.VMEM((nseq, 4, 128, 128), F32)],
        compiler_params=_cparams(("parallel", "arbitrary")),
        name="retention",
    )(P, P, P, P, cos, sin, dm, qin, kend, gs, bd, s0, bo, gn, acc)


def _mlstm_kernel(qk_ref, v_ref, o_ref, if_ref, xcs_ref, cw_ref, cb_ref, bi_ref, bf_ref, tri_ref, bd_ref, eye_ref,
                  c0_ref, n0_ref, m0_ref, bo_ref, gn_ref, _, h_ref, cT_ref, nT_ref, mT_ref, convT_ref,
                  C_scr, n_scr, m_scr, xc_scr, *, nseq, L):
    c = pl.program_id(1)
    rows = nseq * L

    @pl.when(c == 0)
    def _():
        _load_pairs(C_scr, c0_ref, nseq)
        n_scr[...] = n0_ref[...]
        m_scr[...] = m0_ref[...]
        if nseq == 1:
            xc_scr[0:8, :] = jnp.zeros((8, 2 * BR_W), F32)

    cw = cw_ref[...]
    if nseq == 1:
        xc_scr[8:8 + L, :] = qk_ref[...]
        conv = cb_ref[...]
        for w in range(CONV_W):
            conv = conv + xc_scr[5 + w:5 + w + L, :] * cw[w:w + 1, :]
        xc_scr[0:8, :] = xc_scr[L:L + 8, :]
    else:
        parts = []
        for s in range(nseq):
            cs = cb_ref[...]
            for w in range(CONV_W):
                cs = cs + xcs_ref[s, 1 + w:1 + w + L, :] * cw[w:w + 1, :]
            parts.append(cs)
        conv = jnp.concatenate(parts, axis=0)
    qk = conv * _sigmoid(conv)
    q = qk[:, :BR_W]
    k = qk[:, BR_W:] * (HEAD_DIM ** -0.5)
    v = v_ref[...]

    rowi = lax.broadcasted_iota(jnp.int32, (rows, 1), 0)
    tok = rowi % L
    ig = if_ref[:, 0:128] + bi_ref[...]
    fp = if_ref[:, 128:256] + bf_ref[...]
    lf = jnp.minimum(fp, 0.0) - jnp.log(1.0 + jnp.exp(-jnp.abs(fp)))
    tri = tri_ref[...]
    l1, l2, l3 = _split3(lf)
    bcum = _dot(tri, l1) + _dot(tri, l2) + _dot(tri, l3)
    u = ig - bcum
    cm = u
    sh = 1
    while sh < L:
        cm = jnp.maximum(cm, jnp.where(tok >= sh, pltpu.roll(cm, sh, axis=0), -jnp.inf))
        sh *= 2
    if nseq == 1:
        m0r = m_scr[0]
    else:
        m0r = jnp.concatenate([jnp.broadcast_to(m_scr[s], (L, 128)) for s in range(nseq)], axis=0)
    mt = bcum + jnp.maximum(m0r, cm)
    inter = jnp.exp(bcum + m0r - mt)
    bmm = bcum - mt
    emt = jnp.exp(-mt)
    wend_parts = []
    for s in range(nseq):
        last = slice(s * L + L - 1, s * L + L)
        m_end = mt[last, :]
        wend_parts.append(jnp.exp(bcum[last, :] - m_end + u[s * L:(s + 1) * L, :]))
    wend = wend_parts[0] if nseq == 1 else jnp.concatenate(wend_parts, axis=0)
    eye = eye_ref[...]
    u1, u2, u3 = _split3(u)
    ut = _dot_nt(eye, u1) + _dot_nt(eye, u2) + _dot_nt(eye, u3)
    trib = tri > 0.5

    m0m, m1m = _pair_masks()
    lane128 = lax.broadcasted_iota(jnp.int32, (1, 128), 1)
    first = lane128 < HEAD_DIM
    bo = bo_ref[...]
    qn_all = None
    outs = []
    wk_all = []
    for p in range(N_HEADS // 2):
        sl = slice(128 * p, 128 * (p + 1))
        qp, kp, vp = q[:, sl], k[:, sl], v[:, sl]
        kb = kp.astype(BF16)
        num = jnp.zeros((rows, 128), F32)
        dsum = []
        for e, me in enumerate((m0m, m1m)):
            h = 2 * p + e
            logd = bmm[:, h:h + 1] + ut[h:h + 1, :]
            dmat = jnp.where(trib, jnp.exp(jnp.minimum(logd, 0.0)), 0.0)
            sc = _dot_nt((qp * me).astype(BF16), kb) * dmat
            num = num + _dot(sc.astype(BF16), (vp * me).astype(BF16))
            dsum.append(jnp.sum(sc, axis=1, keepdims=True))
        pick = lambda arr: jnp.where(first, arr[:, 2 * p:2 * p + 1], arr[:, 2 * p + 1:2 * p + 2])
        inter_p = pick(inter)
        den = jnp.where(first, dsum[0], dsum[1])
        wend_p = pick(wend)
        qb = qp.astype(BF16)
        kw = kp * wend_p
        wk_all.append(kw)
        qn = jnp.zeros((rows, 128), F32)
        for s in range(nseq):
            C = C_scr[s, p]
            nrow = n_scr[s][:, sl]
            qc = _dot(qb, C.astype(BF16))
            qns = qp * nrow
            if nseq == 1:
                num = num + inter_p * qc
                qn = qns
                kws = kw
                send = inter_p[L - 1:L, :]
            else:
                inseq = (rowi >= s * L) & (rowi < (s + 1) * L)
                num = num + jnp.where(inseq, inter_p * qc, 0.0)
                qn = qn + jnp.where(inseq, qns, 0.0)
                kws = jnp.where(inseq, kw, 0.0)
                send = inter_p[s * L + L - 1:s * L + L, :]
            send_col = jnp.where(lax.broadcasted_iota(jnp.int32, (128, 1), 0) < HEAD_DIM,
                                 send[:, 0:1], send[:, 64:65])
            C_scr[s, p] = C * send_col + bd_ref[...] * _dot_tn(kws.astype(BF16), vp.astype(BF16))
        den = den + inter_p * _segsum_pair(qn, bo)
        outs.append(num / jnp.maximum(jnp.abs(den), pick(emt)))
    hh = jnp.concatenate(outs, axis=1)
    kw_full = jnp.concatenate(wk_all, axis=1)
    for s in range(nseq):
        last = slice(s * L + L - 1, s * L + L)
        send_row = _expand_heads(inter[last, :])
        n_scr[s] = send_row * n_scr[s] + jnp.sum(kw_full[s * L:(s + 1) * L, :], axis=0, keepdims=True)
        m_scr[s] = mt[last, :]
    op = o_ref[...]
    hh = _sigmoid(op) * hh
    h_ref[...] = _head_norm(hh, bo, GN_EPS) * gn_ref[...]

    @pl.when(c == pl.num_programs(1) - 1)
    def _():
        _store_pairs(cT_ref, C_scr, nseq)
        nT_ref[...] = n_scr[...]
        mT_ref[...] = m_scr[...]
        if nseq == 1:
            convT_ref[0] = xc_scr[8 - (CONV_W - 1):8, :]
        else:
            for s in range(nseq):
                convT_ref[s] = xcs_ref[s, 8 - (CONV_W - 1):8, :]


def _segsum_pair(x, bo):
    hi, lo = _split2(x)
    b2 = bo[:128, :128]
    return _dot(hi, b2) + _dot(lo, b2)


def _expand_heads(row):
    lane = lax.broadcasted_iota(jnp.int32, (1, BR_W), 1)
    out = jnp.zeros((1, BR_W), F32)
    for h in range(N_HEADS):
        out = jnp.where((lane >> 6) == h, row[:, h:h + 1], out)
    return out


def _mlstm(P, row0, n_groups, n_chunks, nseq, L, xcs, lp, tabs, c0, l_in, acc, l_out, n0, m0, bo):
    rows = nseq * L
    rb0 = row0 // rows
    tri, bd, eye = tabs
    rowblk = lambda width, cb: pl.BlockSpec((rows, width), lambda i, c: (rb0 + i * n_chunks + c, cb))
    cst = lambda shp: pl.BlockSpec(shp, lambda i, c: (0,) * len(shp))
    stC_in, stC_out = _state_specs(nseq, l_in, l_out)
    stn = pl.BlockSpec((nseq, 1, BR_W), lambda i, c: (i, 0, 0))
    stm = pl.BlockSpec((nseq, 1, 128), lambda i, c: (i, 0, 0))
    if nseq == 1:
        xcs_spec = cst((1, 8, 2 * BR_W))
    else:
        xcs_spec = pl.BlockSpec((nseq, 8, 2 * BR_W), lambda i, c: (i, 0, 0))
    kern = functools.partial(_mlstm_kernel, nseq=nseq, L=L)
    nB = n_groups * nseq
    in_specs = [rowblk(2 * BR_W, COL_ML // (2 * BR_W)), rowblk(BR_W, (COL_ML + 1024) // BR_W),
                rowblk(BR_W, (COL_ML + 1536) // BR_W), rowblk(256, COL_IF // 256), xcs_spec,
                cst((CONV_W, 2 * BR_W)), cst((1, 2 * BR_W)), cst((1, 128)), cst((1, 128)),
                cst((rows, rows)), cst((128, 128)), cst((128, 128)), stC_in, stn, stm, cst((256, 256)),
                cst((1, BR_W)), pl.BlockSpec(memory_space=pl.ANY)]
    return pl.pallas_call(
        kern,
        grid=(n_groups, n_chunks),
        in_specs=in_specs,
        out_specs=[pl.BlockSpec((rows, BR_W), lambda i, c: (i * n_chunks + c, 0)), stC_out, stn, stm,
                   pl.BlockSpec((nseq, CONV_W - 1, 2 * BR_W), lambda i, c: (i, 0, 0))],
        input_output_aliases={len(in_specs) - 1: 1},
        out_shape=[jax.ShapeDtypeStruct((n_groups * n_chunks * rows, BR_W), F32),
                   jax.ShapeDtypeStruct(acc.shape, F32),
                   jax.ShapeDtypeStruct((nB, 1, BR_W), F32),
                   jax.ShapeDtypeStruct((nB, 1, 128), F32),
                   jax.ShapeDtypeStruct((nB, CONV_W - 1, 2 * BR_W), F32)],
        scratch_shapes=[pltpu.VMEM((nseq, 4, 128, 128), F32), pltpu.VMEM((nseq, 1, BR_W), F32),
                        pltpu.VMEM((nseq, 1, 128), F32), pltpu.VMEM((L + 8, 2 * BR_W), F32)],
        compiler_params=_cparams(("parallel", "arbitrary")),
        name="mlstm",
    )(P, P, P, P, xcs, lp['conv_w'], lp['conv_b'], lp['b_i'], lp['b_f'], tri, bd, eye, c0, n0, m0, bo,
      lp['mlstm_gn'], acc)


def _mix_kernel(x_ref, hap_ref, has_ref, hrp_ref, hrs_ref, hmp_ref, hms_ref, g0_ref, g1_ref, g2_ref, wb_ref,
                bg_ref, wo_ref, o_ref, *, n_p):
    def body(h_refs):
        mixed = None
        for n, (h_ref, pg_ref) in enumerate(zip(h_refs, (g0_ref, g1_ref, g2_ref))):
            ub = _dot(h_ref[...].astype(BF16), wb_ref[n])
            t = _sigmoid(pg_ref[...] + bg_ref[n]) * ub
            mixed = t if mixed is None else mixed + t
        o_ref[...] = x_ref[...] + _dot(mixed.astype(BF16), wo_ref[...])

    @pl.when(pl.program_id(0) < n_p)
    def _():
        body((hap_ref, hrp_ref, hmp_ref))

    @pl.when(pl.program_id(0) >= n_p)
    def _():
        body((has_ref, hrs_ref, hms_ref))


def _mix(x, hp, hs, P, wb, bg, wo, tm):
    R = x.shape[0]
    n_p = hp[0].shape[0] // tm
    xs = pl.BlockSpec((tm, D_MODEL), lambda i: (i, 0))
    hps = pl.BlockSpec((tm, BR_W), lambda i: (jnp.minimum(i, n_p - 1), 0))
    hss = pl.BlockSpec((tm, BR_W), lambda i: (jnp.maximum(i - n_p, 0), 0))
    gs = lambda n: pl.BlockSpec((tm, D_MODEL), lambda i: (i, COL_GATE // D_MODEL + n))
    return pl.pallas_call(
        functools.partial(_mix_kernel, n_p=n_p),
        grid=(R // tm,),
        in_specs=[xs, hps, hss, hps, hss, hps, hss, gs(0), gs(1), gs(2),
                  pl.BlockSpec((3, BR_W, D_MODEL), lambda i: (0, 0, 0)),
                  pl.BlockSpec((3, 1, D_MODEL), lambda i: (0, 0, 0)),
                  pl.BlockSpec((D_MODEL, D_MODEL), lambda i: (0, 0))],
        out_specs=xs,
        out_shape=jax.ShapeDtypeStruct((R, D_MODEL), F32),
        compiler_params=_cparams(("parallel",)),
        name="mix_out",
    )(x, hp[0], hs[0], hp[1], hs[1], hp[2], hs[2], P, P, P, wb, bg, wo)


def _ffn_kernel(x_ref, g_ref, w1_ref, w2_ref, gf_ref, o_ref, xn_scr, acc_scr, *, final):
    j = pl.program_id(1)

    @pl.when(j == 0)
    def _():
        xn_scr[...] = _rms(x_ref[...], g_ref[...]).astype(BF16)
        acc_scr[...] = jnp.zeros_like(acc_scr)

    h = jnp.maximum(_dot(xn_scr[...], w1_ref[...]), 0.0)
    acc_scr[...] += _dot((h * h).astype(BF16), w2_ref[...])

    @pl.when(j == pl.num_programs(1) - 1)
    def _():
        y = x_ref[...] + acc_scr[...]
        if final:
            y = _rms(y, gf_ref[...])
        o_ref[...] = y


def _ffn(x, g, w1, w2, gf, tm, tf, final):
    R = x.shape[0]
    kern = functools.partial(_ffn_kernel, final=final)
    return pl.pallas_call(
        kern,
        grid=(R // tm, D_FF // tf),
        in_specs=[pl.BlockSpec((tm, D_MODEL), lambda i, j: (i, 0)),
                  pl.BlockSpec((1, D_MODEL), lambda i, j: (0, 0)),
                  pl.BlockSpec((D_MODEL, tf), lambda i, j: (0, j)),
                  pl.BlockSpec((tf, D_MODEL), lambda i, j: (j, 0)),
                  pl.BlockSpec((1, D_MODEL), lambda i, j: (0, 0))],
        out_specs=pl.BlockSpec((tm, D_MODEL), lambda i, j: (i, 0)),
        out_shape=jax.ShapeDtypeStruct((R, D_MODEL), F32),
        scratch_shapes=[pltpu.VMEM((tm, D_MODEL), BF16), pltpu.VMEM((tm, D_MODEL), F32)],
        compiler_params=_cparams(("parallel", "arbitrary")),
        name="ffn",
    )(x, g, w1, w2, gf)


def _rope_tables(T, pos0):
    half = HEAD_DIM // 2
    inv = jnp.power(ROPE_BASE, -jnp.arange(half, dtype=F32) / half)
    ang = (jnp.arange(T, dtype=F32) + pos0)[:, None] * inv[None, :]
    cos, sin = jnp.cos(ang), jnp.sin(ang)
    cos_h = jnp.concatenate([cos, cos], axis=1)
    sin_h = jnp.concatenate([-sin, sin], axis=1)
    return jnp.tile(cos_h, (1, N_HEADS)), jnp.tile(sin_h, (1, N_HEADS))


def _ret_tables(nseq, L):
    log_g = jnp.log(1.0 - jnp.exp2(-5.0 - jnp.arange(N_HEADS, dtype=F32)))
    idx = jnp.arange(L, dtype=F32)
    diff = idx[:, None] - idx[None, :]
    dmask = jnp.where(diff[None] >= 0, jnp.exp(jnp.maximum(diff, 0.0)[None] * log_g[:, None, None]), 0.0)
    q_in = jnp.exp((idx + 1.0)[:, None] * log_g[None, :])
    k_end = jnp.exp((L - 1.0 - idx)[:, None] * log_g[None, :])
    g_chunk = jnp.exp(L * log_g)
    eye = jnp.eye(nseq, dtype=F32)
    dm = jnp.einsum('st,hij->hsitj', eye, dmask).reshape(N_HEADS, nseq * L, nseq * L)
    qin = jnp.tile(jnp.repeat(q_in, HEAD_DIM, axis=1), (nseq, 1))
    kend = jnp.tile(jnp.repeat(k_end, HEAD_DIM, axis=1), (nseq, 1))
    gs = jnp.broadcast_to(jnp.repeat(g_chunk, HEAD_DIM).reshape(4, 128, 1), (4, 128, 128))
    return dm, qin, kend, gs


def _block_diag_ones(n):
    i = jnp.arange(n) // HEAD_DIM
    return (i[:, None] == i[None, :])


def _seq_tri(nseq, L):
    r = jnp.arange(nseq * L)
    return ((r[:, None] // L == r[None, :] // L) & (r[:, None] >= r[None, :])).astype(BF16)


def _pick_tile(n, cands):
    for c in cands:
        if n % c == 0:
            return c
    raise ValueError(f"no tile for {n}")


def kernel(x_prompt, x_sample, state_rwkv_wkv, state_rwkv_shift, state_ret, state_mlstm_C, state_mlstm_n,
           state_mlstm_m, state_mlstm_conv, norm_mix, w_in, rwkv_mu, rwkv_w0, rwkv_w2, rwkv_a0, rwkv_a2,
           rwkv_g2, rwkv_kk, rwkv_ka, rwkv_rk, rwkv_ln_g, rwkv_ln_b, ret_gn_g, mlstm_conv_w, mlstm_conv_b,
           mlstm_b_i, mlstm_b_f, mlstm_gn_g, w_branch, b_gate, w_out, norm_ffn, w_ff1, w_ff2, norm_final):
    Bp, Tp, _ = x_prompt.shape
    Bs, Ts, _ = x_sample.shape
    Rp, Rs = Bp * Tp, Bs * Ts
    R = Rp + Rs
    past_len = 16384
    Lp = min(PROMPT_CHUNK, Tp)
    assert Tp % Lp == 0 and Bs % SAMPLE_SEQS == 0 and Rp % (SAMPLE_SEQS * Ts) == 0
    assert Tp & (Tp - 1) == 0 and Ts & (Ts - 1) == 0 and Tp >= CONV_W - 1 and Bp % SCAN_BATCH == 0

    x = jnp.concatenate([x_prompt.reshape(Rp, D_MODEL), x_sample.reshape(Rs, D_MODEL)], axis=0)

    o_rw, o_ret, o_ml = 0, RWKV_PROJ, RWKV_PROJ + 4 * BR_W
    o_if = o_ml + 4 * BR_W
    o_gate = o_if + 2 * N_HEADS
    zpad = lambda n: jnp.zeros((DEPTH, D_MODEL, n), BF16)
    w_inb = w_in.astype(BF16)
    w_cat = jnp.concatenate([
        w_inb[:, :, o_ret:o_ret + 4 * BR_W], w_inb[:, :, o_ml:o_ml + 4 * BR_W], w_inb[:, :, o_rw:o_rw + RWKV_PROJ],
        w_inb[:, :, o_if:o_if + N_HEADS], zpad(128 - N_HEADS), w_inb[:, :, o_if + N_HEADS:o_gate],
        zpad(128 - N_HEADS), w_inb[:, :, o_gate:]], axis=2)
    pad_rows = lambda a, top: jnp.concatenate(
        [jnp.zeros((DEPTH, top, BR_W), F32), a, jnp.zeros((DEPTH, 128 - top - a.shape[1], BR_W), F32)], axis=1)
    w2p = pad_rows(rwkv_w2, 0).astype(BF16)
    a2p = pad_rows(rwkv_a2, W_LORA).astype(BF16)
    g2b = rwkv_g2.astype(BF16)
    pad128 = lambda a: jnp.concatenate([a, jnp.zeros((DEPTH, 128 - N_HEADS), F32)], axis=1)
    wbb, wob, w1b, w2b = (t.astype(BF16) for t in (w_branch, w_out, w_ff1, w_ff2))
    bo = _block_diag_ones(256).astype(BF16)
    bd = _block_diag_ones(128).astype(F32)
    eye = jnp.eye(128, dtype=BF16)

    cos_p, sin_p = _rope_tables(Tp, 0)
    cos_s, sin_s = _rope_tables(Ts, past_len)
    cos_s, sin_s = jnp.tile(cos_s, (SAMPLE_SEQS, 1)), jnp.tile(sin_s, (SAMPLE_SEQS, 1))
    rt_p = _ret_tables(1, Lp)
    rt_s = _ret_tables(SAMPLE_SEQS, Ts)
    tri_p, tri_s = _seq_tri(1, Lp), _seq_tri(SAMPLE_SEQS, Ts)

    tm_in = _pick_tile(R, (1536, 768, 576, 192, 64))
    tm_mix = _pick_tile(math.gcd(Rp, Rs), (512, 64))
    tb_p = min(SCAN_TBLK, Tp)

    zeros_heads = jnp.zeros((1, Bp, N_HEADS, HEAD_DIM, HEAD_DIM), F32)
    stacked = lambda B: jnp.zeros((DEPTH, B, N_HEADS, HEAD_DIM, HEAD_DIM), F32)
    wkv_p, ret_p, c_p, wkv_s, ret_s, c_s = (stacked(Bp), stacked(Bp), stacked(Bp),
                                            stacked(Bs), stacked(Bs), stacked(Bs))
    outs = [[] for _ in range(8)]
    for l in range(DEPTH):
        lp = dict(mu=rwkv_mu[l][None], w0=rwkv_w0[l][None], w2=w2p[l], a0=rwkv_a0[l][None], a2=a2p[l], g2=g2b[l],
                  kk=rwkv_kk[l][None], ka=rwkv_ka[l][None], rk=rwkv_rk[l].reshape(1, BR_W),
                  ln_g=rwkv_ln_g[l][None], ln_b=rwkv_ln_b[l][None],
                  conv_w=mlstm_conv_w[l], conv_b=mlstm_conv_b[l][None], b_i=pad128(mlstm_b_i)[l][None],
                  b_f=pad128(mlstm_b_f)[l][None], mlstm_gn=mlstm_gn_g[l][None])
        P = _inproj(x, norm_mix[l][None], w_cat[l], tm_in, 1536)

        fix_s = jnp.repeat(state_rwkv_shift[l], Ts, axis=0)
        ha_p, wkv_p, shift_p = _rwkv(P, 0, Bp, Tp, tb_p, jnp.zeros((8, 128), F32), zeros_heads, 0, wkv_p, l,
                                     lp, bo, False)
        ha_s, wkv_s, shift_s = _rwkv(P, Rp, Bs, Ts, Ts, fix_s, state_rwkv_wkv, l, wkv_s, l, lp, bo, True)
        ha_p = ha_p.reshape(Rp, BR_W)

        gn = ret_gn_g[l][None]
        hr_p, ret_p = _retention(P, 0, Bp, Tp // Lp, 1, Lp, (cos_p, sin_p) + rt_p + (bd,), zeros_heads, 0,
                                 ret_p, l, bo, gn)
        hr_s, ret_s = _retention(P, Rp, Bs // SAMPLE_SEQS, 1, SAMPLE_SEQS, Ts, (cos_s, sin_s) + rt_s + (bd,),
                                 state_ret, l, ret_s, l, bo, gn)

        qk_s = lax.slice(P, (Rp, COL_ML), (R, COL_ML + 2 * BR_W)).reshape(Bs, Ts, 2 * BR_W)
        xcs = jnp.concatenate([jnp.zeros((Bs, 1, 2 * BR_W), F32), state_mlstm_conv[l], qk_s], axis=1)
        hm_p, c_p, n_p, m_p, conv_p = _mlstm(P, 0, Bp, Tp // Lp, 1, Lp, jnp.zeros((1, 8, 2 * BR_W), F32), lp,
                                             (tri_p, bd, eye), zeros_heads, 0, c_p, l,
                                             jnp.zeros((Bp, 1, BR_W), F32), jnp.zeros((Bp, 1, 128), F32), bo)
        m0_s = jnp.concatenate([state_mlstm_m[l], jnp.zeros((Bs, 128 - N_HEADS), F32)], axis=1)[:, None]
        hm_s, c_s, n_s, m_s, conv_s = _mlstm(P, Rp, Bs // SAMPLE_SEQS, 1, SAMPLE_SEQS, Ts, xcs, lp,
                                             (tri_s, bd, eye), state_mlstm_C, l, c_s, l,
                                             state_mlstm_n[l].reshape(Bs, 1, BR_W), m0_s, bo)

        x = _mix(x, (ha_p, hr_p, hm_p), (ha_s, hr_s, hm_s), P, wbb[l], b_gate[l][:, None], wob[l], tm_mix)
        x = _ffn(x, norm_ffn[l][None], w1b[l], w2b[l], norm_final[None], tm_in, 512, l == DEPTH - 1)

        for i, t in enumerate((shift_p, n_p.reshape(Bp, N_HEADS, HEAD_DIM), m_p[:, 0, :N_HEADS], conv_p,
                               shift_s, n_s.reshape(Bs, N_HEADS, HEAD_DIM), m_s[:, 0, :N_HEADS], conv_s)):
            outs[i].append(t)

    y_p = x[:Rp].reshape(Bp, Tp, D_MODEL)
    y_s = x[Rp:].reshape(Bs, Ts, D_MODEL)
    sh_p, n_p, m_p, cv_p, sh_s, n_s, m_s, cv_s = (jnp.stack(o, axis=0) for o in outs)
    return (y_p, y_s, wkv_p, sh_p, ret_p, c_p, n_p, m_p, cv_p, wkv_s, sh_s, ret_s, c_s, n_s, m_s, cv_s)
```

```python
import functools
import math

import jax
import jax.numpy as jnp
from jax import lax
from jax.experimental import pallas as pl
from jax.experimental.pallas import tpu as pltpu

F32 = jnp.float32
BF16 = jnp.bfloat16

D_MODEL = 1024
DEPTH = 4
HEAD_DIM = 64
N_HEADS = 8
BR_W = N_HEADS * HEAD_DIM
W_LORA = 64
A_LORA = 64
G_LORA = 128
CONV_W = 4
D_FF = 4 * D_MODEL
ROPE_BASE = 10000.0
RMS_EPS = 1e-6
RWKV_GN_EPS = 64e-5
GN_EPS = 1e-5
RWKV_PROJ = 3 * BR_W + W_LORA + A_LORA + G_LORA

COL_RET = 0
COL_ML = 2048
COL_RW = 4096
COL_IF = COL_RW + RWKV_PROJ
COL_GATE = 6144
N_PROJ = COL_GATE + 3 * D_MODEL

PROMPT_CHUNK = 128
SAMPLE_SEQS = 8
SCAN_BATCH = 8
SCAN_TBLK = 64
SCAN_SUB = 8
VMEM_LIMIT = 56 * 1024 * 1024


def _cparams(sem):
    return pltpu.CompilerParams(dimension_semantics=sem, vmem_limit_bytes=VMEM_LIMIT)


def _dot(a, b):
    return jnp.dot(a, b, preferred_element_type=F32)


def _dot_nt(a, b):
    return lax.dot_general(a, b, (((1,), (1,)), ((), ())), preferred_element_type=F32)


def _dot_tn(a, b):
    return lax.dot_general(a, b, (((0,), (0,)), ((), ())), preferred_element_type=F32)


def _split2(x):
    hi = x.astype(BF16)
    lo = (x - hi.astype(F32)).astype(BF16)
    return hi, lo


def _split3(x):
    x1 = x.astype(BF16)
    r1 = x - x1.astype(F32)
    x2 = r1.astype(BF16)
    x3 = (r1 - x2.astype(F32)).astype(BF16)
    return x1, x2, x3


def _sigmoid(x):
    return 0.5 * jnp.tanh(0.5 * x) + 0.5


def _softplus(x):
    return jnp.maximum(x, 0.0) + jnp.log(1.0 + jnp.exp(-jnp.abs(x)))


def _seg_bf16(xb, bo):
    return jnp.concatenate([_dot(xb[:, :256], bo), _dot(xb[:, 256:], bo)], axis=1)


def _segsum(x, bo):
    hi, lo = _split2(x)
    return _seg_bf16(hi, bo) + _seg_bf16(lo, bo)


def _head_norm(h, bo, eps):
    mu = _segsum(h, bo) * (1.0 / HEAD_DIM)
    d = h - mu
    var = _segsum(d * d, bo) * (1.0 / HEAD_DIM)
    return d * lax.rsqrt(var + eps)


def _rms(x, g):
    return x * lax.rsqrt(jnp.mean(x * x, axis=-1, keepdims=True) + RMS_EPS) * g


def _inproj_kernel(x_ref, g_ref, w_ref, o_ref, xn_scr):
    @pl.when(pl.program_id(1) == 0)
    def _():
        xn_scr[...] = _rms(x_ref[...], g_ref[...]).astype(BF16)

    o_ref[...] = _dot(xn_scr[...], w_ref[...])


def _inproj(x, g, w, tm, tn):
    R = x.shape[0]
    return pl.pallas_call(
        _inproj_kernel,
        grid=(R // tm, N_PROJ // tn),
        in_specs=[pl.BlockSpec((tm, D_MODEL), lambda i, j: (i, 0)),
                  pl.BlockSpec((1, D_MODEL), lambda i, j: (0, 0)),
                  pl.BlockSpec((D_MODEL, tn), lambda i, j: (0, j))],
        out_specs=pl.BlockSpec((tm, tn), lambda i, j: (i, j)),
        out_shape=jax.ShapeDtypeStruct((R, N_PROJ), F32),
        scratch_shapes=[pltpu.VMEM((tm, D_MODEL), BF16)],
        compiler_params=_cparams(("parallel", "arbitrary")),
        name="inproj",
    )(x, g, w)


def _rwkv_kernel(*refs, nb, tb, n_p, sample):
    p_refs = refs[:n_p]
    (fix_ref, s0_ref, mu_ref, w0_ref, w2_ref, a0_ref, a2_ref, g2_ref, kkp_ref, kap_ref, rk_ref, bo_ref, mk_ref,
     lng_ref, lnb_ref, _, ha_ref, sT_ref, shift_ref, S_scr, seq_scr, o_scr, carry_scr, o8_scr,
     vc_scr) = refs[n_p:]
    sub = min(SCAN_SUB, tb)
    j = pl.program_id(1)
    rows = nb * tb

    @pl.when(j == 0)
    def _():
        for b in range(nb):
            for h in range(N_HEADS):
                S_scr[b, :, h * HEAD_DIM:(h + 1) * HEAD_DIM] = s0_ref[b, h]
        carry_scr[...] = jnp.zeros_like(carry_scr)

    if sample:
        x = p_refs[0][:, :RWKV_PROJ]
        tok = lax.broadcasted_iota(jnp.int32, (rows, 1), 0) & (tb - 1)
        prev = jnp.where(tok == 0, fix_ref[...], pltpu.roll(x, 1, axis=0))
    else:
        first = lax.broadcasted_iota(jnp.int32, (tb, 1), 0) == 0
        xs = [p_refs[b][:, :RWKV_PROJ] for b in range(nb)]
        prev = jnp.concatenate([jnp.where(first, carry_scr[b:b + 1, :], pltpu.roll(xs[b], 1, axis=0))
                                for b in range(nb)], axis=0)
        for b in range(nb):
            carry_scr[b:b + 1, :] = xs[b][tb - 1:tb, :]
        x = jnp.concatenate(xs, axis=0)
    pm = x + (prev - x) * mu_ref[...]
    r = pm[:, 0:512]
    k = pm[:, 512:1024]
    v = pm[:, 1024:1536]
    wa = pm[:, 1536:1664]
    gd = pm[:, 1664:1792]
    bo = bo_ref[...]
    lw = _dot(jnp.tanh(wa).astype(BF16), w2_ref[...])
    w_log = -_softplus(-(w0_ref[...] + lw)) - 0.5
    a = _sigmoid(a0_ref[...] + _dot(wa.astype(BF16), a2_ref[...]))
    kk = k * kkp_ref[...]
    kk = kk * jnp.minimum(lax.rsqrt(jnp.maximum(_segsum(kk * kk, bo), 0.0)), 1e12)
    kt = k * (1.0 + (a - 1.0) * kap_ref[...])
    seq_scr[0] = r
    seq_scr[1] = jnp.exp(-jnp.exp(w_log))
    seq_scr[2] = kt
    seq_scr[3] = -(kk * a)
    seq_scr[4] = kk
    seq_scr[6] = _dot(_sigmoid(gd).astype(BF16), g2_ref[...])
    seq_scr[7] = _segsum(r * kt * rk_ref[...], bo) * v
    v_hi = v.astype(BF16).astype(F32)
    v_lo = _swap_halves(v - v_hi)
    low_half = (lax.broadcasted_iota(jnp.int32, (1, BR_W), 1) & (HEAD_DIM - 1)) < HEAD_DIM // 2
    seq_scr[5] = jnp.where(low_half, v_hi, v_lo)
    seq_scr[8] = jnp.where(low_half, v_lo, v_hi)

    blk = lambda arr, b: arr[b * HEAD_DIM:(b + 1) * HEAD_DIM]
    half = HEAD_DIM // 2
    head_mask = (lax.broadcasted_iota(jnp.int32, (N_HEADS, BR_W), 1) // HEAD_DIM
                 == lax.broadcasted_iota(jnp.int32, (N_HEADS, BR_W), 0)).astype(F32)

    def emit_out(t, s_bf):
        for b in range(nb):
            r8 = (seq_scr[0, pl.ds(b * tb + t, 1), :] * head_mask).astype(BF16)
            o8 = _dot_nt(r8, s_bf[b])
            o8_scr[pl.ds(pl.multiple_of((b * tb + t) * N_HEADS, N_HEADS), N_HEADS), :] = o8

    def step(t, tt):
        row = lambda q, b: seq_scr[q, pl.ds(b * tb + t, 1), :]
        s_bf = [S_scr[b].astype(BF16) for b in range(nb)]
        emit_out(jnp.maximum(t - 1, 0), s_bf)
        sk = _seg_bf16(jnp.concatenate([s_bf[b] * row(4, b).astype(BF16) for b in range(nb)], axis=0), bo)
        for b in range(nb):
            vc = vc_scr[tt, b * HEAD_DIM:(b + 1) * HEAD_DIM, :]
            S_scr[b] = S_scr[b] * row(1, b) + blk(sk, b) * row(3, b) + vc * row(2, b)

    def sub_block(sb, carry):
        t0 = sb * sub
        lhs = []
        for tt in range(sub):
            for b in range(nb):
                lhs.append((seq_scr[5, pl.ds(b * tb + t0 + tt, 1), :] * mk_ref[:half]).astype(BF16))
                lhs.append((seq_scr[8, pl.ds(b * tb + t0 + tt, 1), :] * mk_ref[half:]).astype(BF16))
        vc_all = _seg_bf16(jnp.concatenate(lhs, axis=0), bo)
        for tt in range(sub):
            vc_scr[tt] = vc_all[tt * nb * HEAD_DIM:(tt + 1) * nb * HEAD_DIM]
        for tt in range(sub):
            step(t0 + tt, tt)
        return carry

    lax.fori_loop(0, tb // sub, sub_block, 0)
    emit_out(tb - 1, [S_scr[b].astype(BF16) for b in range(nb)])

    for h in range(N_HEADS):
        o_scr[:, h * HEAD_DIM:(h + 1) * HEAD_DIM] = o8_scr[pl.ds(h, rows, stride=N_HEADS), :]

    o = _head_norm(o_scr[...], bo, RWKV_GN_EPS) * lng_ref[...] + lnb_ref[...]
    ha = (o + seq_scr[7]) * seq_scr[6]
    if sample:
        ha_ref[...] = ha
    else:
        for b in range(nb):
            ha_ref[b] = ha[b * tb:(b + 1) * tb]

    @pl.when(j == pl.num_programs(1) - 1)
    def _():
        for b in range(nb):
            for h in range(N_HEADS):
                sT_ref[b, h] = S_scr[b, :, h * HEAD_DIM:(h + 1) * HEAD_DIM]
            shift_ref[b:b + 1, :] = x[b * tb + tb - 1:b * tb + tb, :]


def _rwkv(P, row0, B, T, tb, fix, s0, l_in, acc, l_out, lp, bo, sample):
    nb = SCAN_BATCH
    cb = COL_RW // 2048
    if sample:
        assert tb == T
        p_specs = [pl.BlockSpec((nb * T, 2048), lambda i, j: (row0 // (nb * T) + i, cb))]
        fix_spec = pl.BlockSpec((nb * T, RWKV_PROJ), lambda i, j: (i, 0))
        ha_spec = pl.BlockSpec((nb * T, BR_W), lambda i, j: (i, 0))
        ha_shape = jax.ShapeDtypeStruct((B * T, BR_W), F32)
    else:
        p_specs = [pl.BlockSpec((tb, 2048), lambda i, j, b=b: (row0 // tb + (i * nb + b) * (T // tb) + j, cb))
                   for b in range(nb)]
        fix_spec = pl.BlockSpec(fix.shape, lambda i, j: (0, 0))
        ha_spec = pl.BlockSpec((nb, tb, BR_W), lambda i, j: (i, j, 0))
        ha_shape = jax.ShapeDtypeStruct((B, T, BR_W), F32)
    row = lambda n: pl.BlockSpec((1, n), lambda i, j: (0, 0))
    full = lambda a, b: pl.BlockSpec((a, b), lambda i, j: (0, 0))
    st_in, st_out = _state_specs(nb, l_in, l_out)
    kern = functools.partial(_rwkv_kernel, nb=nb, tb=tb, n_p=len(p_specs), sample=sample)
    in_specs = p_specs + [fix_spec, st_in, row(RWKV_PROJ), row(BR_W), full(128, BR_W), row(BR_W), full(128, BR_W),
                          full(128, BR_W), row(BR_W), row(BR_W), row(BR_W), full(256, 256),
                          full(HEAD_DIM, BR_W), row(BR_W), row(BR_W), pl.BlockSpec(memory_space=pl.ANY)]
    return pl.pallas_call(
        kern,
        grid=(B // nb, T // tb),
        in_specs=in_specs,
        out_specs=[ha_spec, st_out, pl.BlockSpec((nb, RWKV_PROJ), lambda i, j: (i, 0))],
        out_shape=[ha_shape, jax.ShapeDtypeStruct(acc.shape, F32), jax.ShapeDtypeStruct((B, RWKV_PROJ), F32)],
        input_output_aliases={len(in_specs) - 1: 1},
        scratch_shapes=[pltpu.VMEM((nb, HEAD_DIM, BR_W), F32), pltpu.VMEM((9, nb * tb, BR_W), F32),
                        pltpu.VMEM((nb * tb, BR_W), F32), pltpu.VMEM((nb, RWKV_PROJ), F32),
                        pltpu.VMEM((nb * tb * N_HEADS, HEAD_DIM), F32),
                        pltpu.VMEM((min(SCAN_SUB, tb), nb * HEAD_DIM, BR_W), F32)],
        compiler_params=_cparams(("parallel", "arbitrary")),
        name="rwkv",
    )(*([P] * len(p_specs)), fix, s0, lp['mu'], lp['w0'], lp['w2'], lp['a0'], lp['a2'], lp['g2'], lp['kk'],
      lp['ka'], lp['rk'], bo, _diag_mask(), lp['ln_g'], lp['ln_b'], acc)


def _state_specs(nseq, l_in, l_out):
    mk = lambda l: pl.BlockSpec((None, nseq, N_HEADS, HEAD_DIM, HEAD_DIM), lambda i, c: (l, i, 0, 0, 0))
    return mk(l_in), mk(l_out)


def _diag_mask():
    u = jnp.arange(HEAD_DIM)[:, None]
    j = jnp.arange(BR_W)[None, :] % HEAD_DIM
    return ((j == u) | (j == (u + HEAD_DIM // 2) % HEAD_DIM)).astype(F32)


def _pair_masks():
    lane = lax.broadcasted_iota(jnp.int32, (1, 128), 1)
    m0 = (lane < HEAD_DIM).astype(F32)
    return m0, 1.0 - m0


def _swap_halves(x):
    lane = lax.broadcasted_iota(jnp.int32, x.shape, 1)
    n = x.shape[1]
    return jnp.where((lane & 63) < 32, pltpu.roll(x, n - 32, axis=1), pltpu.roll(x, 32, axis=1))


def _ret_kernel(q_ref, k_ref, v_ref, z_ref, cos_ref, sin_ref, dm_ref, qin_ref, kend_ref, gs_ref, bd_ref, s0_ref,
                bo_ref, gn_ref, _, h_ref, sT_ref, S_scr, *, nseq, L):
    c = pl.program_id(1)

    @pl.when(c == 0)
    def _():
        _load_pairs(S_scr, s0_ref, nseq)

    cos = cos_ref[...]
    sin = sin_ref[...]
    q = q_ref[...]
    k = k_ref[...]
    q = q * cos + _swap_halves(q) * sin
    k = (k * cos + _swap_halves(k) * sin) * (HEAD_DIM ** -0.5)
    v = v_ref[...]
    qin = qin_ref[...]
    ke = k * kend_ref[...]
    m0, m1 = _pair_masks()
    rows = nseq * L
    rowi = lax.broadcasted_iota(jnp.int32, (rows, 1), 0)
    outs = []
    for p in range(N_HEADS // 2):
        sl = slice(128 * p, 128 * (p + 1))
        qp, kp, vp, kep = q[:, sl], k[:, sl], v[:, sl], ke[:, sl]
        kb = kp.astype(BF16)
        o = jnp.zeros((rows, 128), F32)
        for e, me in enumerate((m0, m1)):
            sc = _dot_nt((qp * me).astype(BF16), kb) * dm_ref[2 * p + e]
            o = o + _dot(sc.astype(BF16), (vp * me).astype(BF16))
        qb = qp.astype(BF16)
        for s in range(nseq):
            S = S_scr[s, p]
            oi = _dot(qb, S.astype(BF16)) * qin[:, sl]
            if nseq == 1:
                o = o + oi
                kes = kep
            else:
                inseq = (rowi >= s * L) & (rowi < (s + 1) * L)
                o = o + jnp.where(inseq, oi, 0.0)
                kes = jnp.where(inseq, kep, 0.0)
            S_scr[s, p] = S * gs_ref[p] + bd_ref[...] * _dot_tn(kes.astype(BF16), vp.astype(BF16))
        outs.append(o)
    o = jnp.concatenate(outs, axis=1)
    o = _head_norm(o, bo_ref[...], GN_EPS) * gn_ref[...]
    z = z_ref[...]
    h_ref[...] = z * _sigmoid(z) * o

    @pl.when(c == pl.num_programs(1) - 1)
    def _():
        _store_pairs(sT_ref, S_scr, nseq)


def _load_pairs(S_scr, s0_ref, nseq):
    S_scr[...] = jnp.zeros_like(S_scr)
    for s in range(nseq):
        for p in range(N_HEADS // 2):
            S_scr[s, p, 0:HEAD_DIM, 0:HEAD_DIM] = s0_ref[s, 2 * p]
            S_scr[s, p, HEAD_DIM:, HEAD_DIM:] = s0_ref[s, 2 * p + 1]


def _store_pairs(sT_ref, S_scr, nseq):
    for s in range(nseq):
        for p in range(N_HEADS // 2):
            sT_ref[s, 2 * p] = S_scr[s, p, 0:HEAD_DIM, 0:HEAD_DIM]
            sT_ref[s, 2 * p + 1] = S_scr[s, p, HEAD_DIM:, HEAD_DIM:]


def _retention(P, row0, n_groups, n_chunks, nseq, L, tabs, s0, l_in, acc, l_out, bo, gn):
    rows = nseq * L
    rb0 = row0 // rows
    cos, sin, dm, qin, kend, gs, bd = tabs
    col = lambda cb: pl.BlockSpec((rows, BR_W), lambda i, c: (rb0 + i * n_chunks + c, cb))
    tab = pl.BlockSpec((rows, BR_W), lambda i, c: (c, 0))
    cst = lambda shp: pl.BlockSpec(shp, lambda i, c: (0,) * len(shp))
    st_in, st_out = _state_specs(nseq, l_in, l_out)
    kern = functools.partial(_ret_kernel, nseq=nseq, L=L)
    in_specs = [col(0), col(1), col(2), col(3), tab, tab, cst((N_HEADS, rows, rows)), cst((rows, BR_W)),
                cst((rows, BR_W)), cst((4, 128, 128)), cst((128, 128)), st_in, cst((256, 256)), cst((1, BR_W)),
                pl.BlockSpec(memory_space=pl.ANY)]
    return pl.pallas_call(
        kern,
        grid=(n_groups, n_chunks),
        in_specs=in_specs,
        out_specs=[pl.BlockSpec((rows, BR_W), lambda i, c: (i * n_chunks + c, 0)), st_out],
        out_shape=[jax.ShapeDtypeStruct((n_groups * n_chunks * rows, BR_W), F32),
                   jax.ShapeDtypeStruct(acc.shape, F32)],
        input_output_aliases={len(in_specs) - 1: 1},
        scratch_shapes=[pltpu.VMEM((nseq, 4, 128, 128), F32)],
        compiler_params=_cparams(("parallel", "arbitrary")),
        name="retention",
    )(P, P, P, P, cos, sin, dm, qin, kend, gs, bd, s0, bo, gn, acc)


def _mlstm_kernel(qk_ref, v_ref, o_ref, if_ref, xcs_ref, cw_ref, cb_ref, bi_ref, bf_ref, tri_ref, bd_ref, eye_ref,
                  c0_ref, n0_ref, m0_ref, bo_ref, gn_ref, _, h_ref, cT_ref, nT_ref, mT_ref, convT_ref,
                  C_scr, n_scr, m_scr, xc_scr, *, nseq, L):
    c = pl.program_id(1)
    rows = nseq * L

    @pl.when(c == 0)
    def _():
        _load_pairs(C_scr, c0_ref, nseq)
        n_scr[...] = n0_ref[...]
        m_scr[...] = m0_ref[...]
        if nseq == 1:
            xc_scr[0:8, :] = jnp.zeros((8, 2 * BR_W), F32)

    cw = cw_ref[...]
    if nseq == 1:
        xc_scr[8:8 + L, :] = qk_ref[...]
        conv = cb_ref[...]
        for w in range(CONV_W):
            conv = conv + xc_scr[5 + w:5 + w + L, :] * cw[w:w + 1, :]
        xc_scr[0:8, :] = xc_scr[L:L + 8, :]
    else:
        parts = []
        for s in range(nseq):
            cs = cb_ref[...]
            for w in range(CONV_W):
                cs = cs + xcs_ref[s, 1 + w:1 + w + L, :] * cw[w:w + 1, :]
            parts.append(cs)
        conv = jnp.concatenate(parts, axis=0)
    qk = conv * _sigmoid(conv)
    q = qk[:, :BR_W]
    k = qk[:, BR_W:] * (HEAD_DIM ** -0.5)
    v = v_ref[...]

    rowi = lax.broadcasted_iota(jnp.int32, (rows, 1), 0)
    tok = rowi % L
    ig = if_ref[:, 0:128] + bi_ref[...]
    fp = if_ref[:, 128:256] + bf_ref[...]
    lf = jnp.minimum(fp, 0.0) - jnp.log(1.0 + jnp.exp(-jnp.abs(fp)))
    tri = tri_ref[...]
    l1, l2, l3 = _split3(lf)
    bcum = _dot(tri, l1) + _dot(tri, l2) + _dot(tri, l3)
    u = ig - bcum
    cm = u
    sh = 1
    while sh < L:
        cm = jnp.maximum(cm, jnp.where(tok >= sh, pltpu.roll(cm, sh, axis=0), -jnp.inf))
        sh *= 2
    if nseq == 1:
        m0r = m_scr[0]
    else:
        m0r = jnp.concatenate([jnp.broadcast_to(m_scr[s], (L, 128)) for s in range(nseq)], axis=0)
    mt = bcum + jnp.maximum(m0r, cm)
    inter = jnp.exp(bcum + m0r - mt)
    bmm = bcum - mt
    emt = jnp.exp(-mt)
    wend_parts = []
    for s in range(nseq):
        last = slice(s * L + L - 1, s * L + L)
        m_end = mt[last, :]
        wend_parts.append(jnp.exp(bcum[last, :] - m_end + u[s * L:(s + 1) * L, :]))
    wend = wend_parts[0] if nseq == 1 else jnp.concatenate(wend_parts, axis=0)
    eye = eye_ref[...]
    u1, u2, u3 = _split3(u)
    ut = _dot_nt(eye, u1) + _dot_nt(eye, u2) + _dot_nt(eye, u3)
    trib = tri > 0.5

    m0m, m1m = _pair_masks()
    lane128 = lax.broadcasted_iota(jnp.int32, (1, 128), 1)
    first = lane128 < HEAD_DIM
    bo = bo_ref[...]
    qn_all = None
    outs = []
    wk_all = []
    for p in range(N_HEADS // 2):
        sl = slice(128 * p, 128 * (p + 1))
        qp, kp, vp = q[:, sl], k[:, sl], v[:, sl]
        kb = kp.astype(BF16)
        num = jnp.zeros((rows, 128), F32)
        dsum = []
        for e, me in enumerate((m0m, m1m)):
            h = 2 * p + e
            logd = bmm[:, h:h + 1] + ut[h:h + 1, :]
            dmat = jnp.where(trib, jnp.exp(jnp.minimum(logd, 0.0)), 0.0)
            sc = _dot_nt((qp * me).astype(BF16), kb) * dmat
            num = num + _dot(sc.astype(BF16), (vp * me).astype(BF16))
            dsum.append(jnp.sum(sc, axis=1, keepdims=True))
        pick = lambda arr: jnp.where(first, arr[:, 2 * p:2 * p + 1], arr[:, 2 * p + 1:2 * p + 2])
        inter_p = pick(inter)
        den = jnp.where(first, dsum[0], dsum[1])
        wend_p = pick(wend)
        qb = qp.astype(BF16)
        kw = kp * wend_p
        wk_all.append(kw)
        qn = jnp.zeros((rows, 128), F32)
        for s in range(nseq):
            C = C_scr[s, p]
            nrow = n_scr[s][:, sl]
            qc = _dot(qb, C.astype(BF16))
            qns = qp * nrow
            if nseq == 1:
                num = num + inter_p * qc
                qn = qns
                kws = kw
                send = inter_p[L - 1:L, :]
            else:
                inseq = (rowi >= s * L) & (rowi < (s + 1) * L)
                num = num + jnp.where(inseq, inter_p * qc, 0.0)
                qn = qn + jnp.where(inseq, qns, 0.0)
                kws = jnp.where(inseq, kw, 0.0)
                send = inter_p[s * L + L - 1:s * L + L, :]
            send_col = jnp.where(lax.broadcasted_iota(jnp.int32, (128, 1), 0) < HEAD_DIM,
                                 send[:, 0:1], send[:, 64:65])
            C_scr[s, p] = C * send_col + bd_ref[...] * _dot_tn(kws.astype(BF16), vp.astype(BF16))
        den = den + inter_p * _segsum_pair(qn, bo)
        outs.append(num / jnp.maximum(jnp.abs(den), pick(emt)))
    hh = jnp.concatenate(outs, axis=1)
    kw_full = jnp.concatenate(wk_all, axis=1)
    for s in range(nseq):
        last = slice(s * L + L - 1, s * L + L)
        send_row = _expand_heads(inter[last, :])
        n_scr[s] = send_row * n_scr[s] + jnp.sum(kw_full[s * L:(s + 1) * L, :], axis=0, keepdims=True)
        m_scr[s] = mt[last, :]
    op = o_ref[...]
    hh = _sigmoid(op) * hh
    h_ref[...] = _head_norm(hh, bo, GN_EPS) * gn_ref[...]

    @pl.when(c == pl.num_programs(1) - 1)
    def _():
        _store_pairs(cT_ref, C_scr, nseq)
        nT_ref[...] = n_scr[...]
        mT_ref[...] = m_scr[...]
        if nseq == 1:
            convT_ref[0] = xc_scr[8 - (CONV_W - 1):8, :]
        else:
            for s in range(nseq):
                convT_ref[s] = xcs_ref[s, 8 - (CONV_W - 1):8, :]


def _segsum_pair(x, bo):
    hi, lo = _split2(x)
    b2 = bo[:128, :128]
    return _dot(hi, b2) + _dot(lo, b2)


def _expand_heads(row):
    lane = lax.broadcasted_iota(jnp.int32, (1, BR_W), 1)
    out = jnp.zeros((1, BR_W), F32)
    for h in range(N_HEADS):
        out = jnp.where((lane >> 6) == h, row[:, h:h + 1], out)
    return out


def _mlstm(P, row0, n_groups, n_chunks, nseq, L, xcs, lp, tabs, c0, l_in, acc, l_out, n0, m0, bo):
    rows = nseq * L
    rb0 = row0 // rows
    tri, bd, eye = tabs
    rowblk = lambda width, cb: pl.BlockSpec((rows, width), lambda i, c: (rb0 + i * n_chunks + c, cb))
    cst = lambda shp: pl.BlockSpec(shp, lambda i, c: (0,) * len(shp))
    stC_in, stC_out = _state_specs(nseq, l_in, l_out)
    stn = pl.BlockSpec((nseq, 1, BR_W), lambda i, c: (i, 0, 0))
    stm = pl.BlockSpec((nseq, 1, 128), lambda i, c: (i, 0, 0))
    if nseq == 1:
        xcs_spec = cst((1, 8, 2 * BR_W))
    else:
        xcs_spec = pl.BlockSpec((nseq, 8, 2 * BR_W), lambda i, c: (i, 0, 0))
    kern = functools.partial(_mlstm_kernel, nseq=nseq, L=L)
    nB = n_groups * nseq
    in_specs = [rowblk(2 * BR_W, COL_ML // (2 * BR_W)), rowblk(BR_W, (COL_ML + 1024) // BR_W),
                rowblk(BR_W, (COL_ML + 1536) // BR_W), rowblk(256, COL_IF // 256), xcs_spec,
                cst((CONV_W, 2 * BR_W)), cst((1, 2 * BR_W)), cst((1, 128)), cst((1, 128)),
                cst((rows, rows)), cst((128, 128)), cst((128, 128)), stC_in, stn, stm, cst((256, 256)),
                cst((1, BR_W)), pl.BlockSpec(memory_space=pl.ANY)]
    return pl.pallas_call(
        kern,
        grid=(n_groups, n_chunks),
        in_specs=in_specs,
        out_specs=[pl.BlockSpec((rows, BR_W), lambda i, c: (i * n_chunks + c, 0)), stC_out, stn, stm,
                   pl.BlockSpec((nseq, CONV_W - 1, 2 * BR_W), lambda i, c: (i, 0, 0))],
        input_output_aliases={len(in_specs) - 1: 1},
        out_shape=[jax.ShapeDtypeStruct((n_groups * n_chunks * rows, BR_W), F32),
                   jax.ShapeDtypeStruct(acc.shape, F32),
                   jax.ShapeDtypeStruct((nB, 1, BR_W), F32),
                   jax.ShapeDtypeStruct((nB, 1, 128), F32),
                   jax.ShapeDtypeStruct((nB, CONV_W - 1, 2 * BR_W), F32)],
        scratch_shapes=[pltpu.VMEM((nseq, 4, 128, 128), F32), pltpu.VMEM((nseq, 1, BR_W), F32),
                        pltpu.VMEM((nseq, 1, 128), F32), pltpu.VMEM((L + 8, 2 * BR_W), F32)],
        compiler_params=_cparams(("parallel", "arbitrary")),
        name="mlstm",
    )(P, P, P, P, xcs, lp['conv_w'], lp['conv_b'], lp['b_i'], lp['b_f'], tri, bd, eye, c0, n0, m0, bo,
      lp['mlstm_gn'], acc)


def _mix_kernel(x_ref, hap_ref, has_ref, hrp_ref, hrs_ref, hmp_ref, hms_ref, g0_ref, g1_ref, g2_ref, wb_ref,
                bg_ref, wo_ref, o_ref, *, n_p):
    def body(h_refs):
        mixed = None
        for n, (h_ref, pg_ref) in enumerate(zip(h_refs, (g0_ref, g1_ref, g2_ref))):
            ub = _dot(h_ref[...].astype(BF16), wb_ref[n])
            t = _sigmoid(pg_ref[...] + bg_ref[n]) * ub
            mixed = t if mixed is None else mixed + t
        o_ref[...] = x_ref[...] + _dot(mixed.astype(BF16), wo_ref[...])

    @pl.when(pl.program_id(0) < n_p)
    def _():
        body((hap_ref, hrp_ref, hmp_ref))

    @pl.when(pl.program_id(0) >= n_p)
    def _():
        body((has_ref, hrs_ref, hms_ref))


def _mix(x, hp, hs, P, wb, bg, wo, tm):
    R = x.shape[0]
    n_p = hp[0].shape[0] // tm
    xs = pl.BlockSpec((tm, D_MODEL), lambda i: (i, 0))
    hps = pl.BlockSpec((tm, BR_W), lambda i: (jnp.minimum(i, n_p - 1), 0))
    hss = pl.BlockSpec((tm, BR_W), lambda i: (jnp.maximum(i - n_p, 0), 0))
    gs = lambda n: pl.BlockSpec((tm, D_MODEL), lambda i: (i, COL_GATE // D_MODEL + n))
    return pl.pallas_call(
        functools.partial(_mix_kernel, n_p=n_p),
        grid=(R // tm,),
        in_specs=[xs, hps, hss, hps, hss, hps, hss, gs(0), gs(1), gs(2),
                  pl.BlockSpec((3, BR_W, D_MODEL), lambda i: (0, 0, 0)),
                  pl.BlockSpec((3, 1, D_MODEL), lambda i: (0, 0, 0)),
                  pl.BlockSpec((D_MODEL, D_MODEL), lambda i: (0, 0))],
        out_specs=xs,
        out_shape=jax.ShapeDtypeStruct((R, D_MODEL), F32),
        compiler_params=_cparams(("parallel",)),
        name="mix_out",
    )(x, hp[0], hs[0], hp[1], hs[1], hp[2], hs[2], P, P, P, wb, bg, wo)


def _ffn_kernel(x_ref, g_ref, w1_ref, w2_ref, gf_ref, o_ref, xn_scr, acc_scr, *, final):
    j = pl.program_id(1)

    @pl.when(j == 0)
    def _():
        xn_scr[...] = _rms(x_ref[...], g_ref[...]).astype(BF16)
        acc_scr[...] = jnp.zeros_like(acc_scr)

    h = jnp.maximum(_dot(xn_scr[...], w1_ref[...]), 0.0)
    acc_scr[...] += _dot((h * h).astype(BF16), w2_ref[...])

    @pl.when(j == pl.num_programs(1) - 1)
    def _():
        y = x_ref[...] + acc_scr[...]
        if final:
            y = _rms(y, gf_ref[...])
        o_ref[...] = y


def _ffn(x, g, w1, w2, gf, tm, tf, final):
    R = x.shape[0]
    kern = functools.partial(_ffn_kernel, final=final)
    return pl.pallas_call(
        kern,
        grid=(R // tm, D_FF // tf),
        in_specs=[pl.BlockSpec((tm, D_MODEL), lambda i, j: (i, 0)),
                  pl.BlockSpec((1, D_MODEL), lambda i, j: (0, 0)),
                  pl.BlockSpec((D_MODEL, tf), lambda i, j: (0, j)),
                  pl.BlockSpec((tf, D_MODEL), lambda i, j: (j, 0)),
                  pl.BlockSpec((1, D_MODEL), lambda i, j: (0, 0))],
        out_specs=pl.BlockSpec((tm, D_MODEL), lambda i, j: (i, 0)),
        out_shape=jax.ShapeDtypeStruct((R, D_MODEL), F32),
        scratch_shapes=[pltpu.VMEM((tm, D_MODEL), BF16), pltpu.VMEM((tm, D_MODEL), F32)],
        compiler_params=_cparams(("parallel", "arbitrary")),
        name="ffn",
    )(x, g, w1, w2, gf)


def _rope_tables(T, pos0):
    half = HEAD_DIM // 2
    inv = jnp.power(ROPE_BASE, -jnp.arange(half, dtype=F32) / half)
    ang = (jnp.arange(T, dtype=F32) + pos0)[:, None] * inv[None, :]
    cos, sin = jnp.cos(ang), jnp.sin(ang)
    cos_h = jnp.concatenate([cos, cos], axis=1)
    sin_h = jnp.concatenate([-sin, sin], axis=1)
    return jnp.tile(cos_h, (1, N_HEADS)), jnp.tile(sin_h, (1, N_HEADS))


def _ret_tables(nseq, L):
    log_g = jnp.log(1.0 - jnp.exp2(-5.0 - jnp.arange(N_HEADS, dtype=F32)))
    idx = jnp.arange(L, dtype=F32)
    diff = idx[:, None] - idx[None, :]
    dmask = jnp.where(diff[None] >= 0, jnp.exp(jnp.maximum(diff, 0.0)[None] * log_g[:, None, None]), 0.0)
    q_in = jnp.exp((idx + 1.0)[:, None] * log_g[None, :])
    k_end = jnp.exp((L - 1.0 - idx)[:, None] * log_g[None, :])
    g_chunk = jnp.exp(L * log_g)
    eye = jnp.eye(nseq, dtype=F32)
    dm = jnp.einsum('st,hij->hsitj', eye, dmask).reshape(N_HEADS, nseq * L, nseq * L)
    qin = jnp.tile(jnp.repeat(q_in, HEAD_DIM, axis=1), (nseq, 1))
    kend = jnp.tile(jnp.repeat(k_end, HEAD_DIM, axis=1), (nseq, 1))
    gs = jnp.broadcast_to(jnp.repeat(g_chunk, HEAD_DIM).reshape(4, 128, 1), (4, 128, 128))
    return dm, qin, kend, gs


def _block_diag_ones(n):
    i = jnp.arange(n) // HEAD_DIM
    return (i[:, None] == i[None, :])


def _seq_tri(nseq, L):
    r = jnp.arange(nseq * L)
    return ((r[:, None] // L == r[None, :] // L) & (r[:, None] >= r[None, :])).astype(BF16)


def _pick_tile(n, cands):
    for c in cands:
        if n % c == 0:
            return c
    raise ValueError(f"no tile for {n}")


def kernel(x_prompt, x_sample, state_rwkv_wkv, state_rwkv_shift, state_ret, state_mlstm_C, state_mlstm_n,
           state_mlstm_m, state_mlstm_conv, norm_mix, w_in, rwkv_mu, rwkv_w0, rwkv_w2, rwkv_a0, rwkv_a2,
           rwkv_g2, rwkv_kk, rwkv_ka, rwkv_rk, rwkv_ln_g, rwkv_ln_b, ret_gn_g, mlstm_conv_w, mlstm_conv_b,
           mlstm_b_i, mlstm_b_f, mlstm_gn_g, w_branch, b_gate, w_out, norm_ffn, w_ff1, w_ff2, norm_final):
    Bp, Tp, _ = x_prompt.shape
    Bs, Ts, _ = x_sample.shape
    Rp, Rs = Bp * Tp, Bs * Ts
    R = Rp + Rs
    past_len = 16384
    Lp = min(PROMPT_CHUNK, Tp)
    assert Tp % Lp == 0 and Bs % SAMPLE_SEQS == 0 and Rp % (SAMPLE_SEQS * Ts) == 0
    assert Tp & (Tp - 1) == 0 and Ts & (Ts - 1) == 0 and Tp >= CONV_W - 1 and Bp % SCAN_BATCH == 0

    x = jnp.concatenate([x_prompt.reshape(Rp, D_MODEL), x_sample.reshape(Rs, D_MODEL)], axis=0)

    o_rw, o_ret, o_ml = 0, RWKV_PROJ, RWKV_PROJ + 4 * BR_W
    o_if = o_ml + 4 * BR_W
    o_gate = o_if + 2 * N_HEADS
    zpad = lambda n: jnp.zeros((DEPTH, D_MODEL, n), BF16)
    w_inb = w_in.astype(BF16)
    w_cat = jnp.concatenate([
        w_inb[:, :, o_ret:o_ret + 4 * BR_W], w_inb[:, :, o_ml:o_ml + 4 * BR_W], w_inb[:, :, o_rw:o_rw + RWKV_PROJ],
        w_inb[:, :, o_if:o_if + N_HEADS], zpad(128 - N_HEADS), w_inb[:, :, o_if + N_HEADS:o_gate],
        zpad(128 - N_HEADS), w_inb[:, :, o_gate:]], axis=2)
    pad_rows = lambda a, top: jnp.concatenate(
        [jnp.zeros((DEPTH, top, BR_W), F32), a, jnp.zeros((DEPTH, 128 - top - a.shape[1], BR_W), F32)], axis=1)
    w2p = pad_rows(rwkv_w2, 0).astype(BF16)
    a2p = pad_rows(rwkv_a2, W_LORA).astype(BF16)
    g2b = rwkv_g2.astype(BF16)
    pad128 = lambda a: jnp.concatenate([a, jnp.zeros((DEPTH, 128 - N_HEADS), F32)], axis=1)
    wbb, wob, w1b, w2b = (t.astype(BF16) for t in (w_branch, w_out, w_ff1, w_ff2))
    bo = _block_diag_ones(256).astype(BF16)
    bd = _block_diag_ones(128).astype(F32)
    eye = jnp.eye(128, dtype=BF16)

    cos_p, sin_p = _rope_tables(Tp, 0)
    cos_s, sin_s = _rope_tables(Ts, past_len)
    cos_s, sin_s = jnp.tile(cos_s, (SAMPLE_SEQS, 1)), jnp.tile(sin_s, (SAMPLE_SEQS, 1))
    rt_p = _ret_tables(1, Lp)
    rt_s = _ret_tables(SAMPLE_SEQS, Ts)
    tri_p, tri_s = _seq_tri(1, Lp), _seq_tri(SAMPLE_SEQS, Ts)

    tm_in = _pick_tile(R, (1536, 768, 576, 192, 64))
    tm_mix = _pick_tile(math.gcd(Rp, Rs), (512, 64))
    tb_p = min(SCAN_TBLK, Tp)

    zeros_heads = jnp.zeros((1, Bp, N_HEADS, HEAD_DIM, HEAD_DIM), F32)
    stacked = lambda B: jnp.zeros((DEPTH, B, N_HEADS, HEAD_DIM, HEAD_DIM), F32)
    wkv_p, ret_p, c_p, wkv_s, ret_s, c_s = (stacked(Bp), stacked(Bp), stacked(Bp),
                                            stacked(Bs), stacked(Bs), stacked(Bs))
    outs = [[] for _ in range(8)]
    for l in range(DEPTH):
        lp = dict(mu=rwkv_mu[l][None], w0=rwkv_w0[l][None], w2=w2p[l], a0=rwkv_a0[l][None], a2=a2p[l], g2=g2b[l],
                  kk=rwkv_kk[l][None], ka=rwkv_ka[l][None], rk=rwkv_rk[l].reshape(1, BR_W),
                  ln_g=rwkv_ln_g[l][None], ln_b=rwkv_ln_b[l][None],
                  conv_w=mlstm_conv_w[l], conv_b=mlstm_conv_b[l][None], b_i=pad128(mlstm_b_i)[l][None],
                  b_f=pad128(mlstm_b_f)[l][None], mlstm_gn=mlstm_gn_g[l][None])
        P = _inproj(x, norm_mix[l][None], w_cat[l], tm_in, 1536)

        fix_s = jnp.repeat(state_rwkv_shift[l], Ts, axis=0)
        ha_p, wkv_p, shift_p = _rwkv(P, 0, Bp, Tp, tb_p, jnp.zeros((8, 128), F32), zeros_heads, 0, wkv_p, l,
                                     lp, bo, False)
        ha_s, wkv_s, shift_s = _rwkv(P, Rp, Bs, Ts, Ts, fix_s, state_rwkv_wkv[l][None], 0, wkv_s, l, lp, bo, True)
        ha_p = ha_p.reshape(Rp, BR_W)

        gn = ret_gn_g[l][None]
        hr_p, ret_p = _retention(P, 0, Bp, Tp // Lp, 1, Lp, (cos_p, sin_p) + rt_p + (bd,), zeros_heads, 0,
                                 ret_p, l, bo, gn)
        hr_s, ret_s = _retention(P, Rp, Bs // SAMPLE_SEQS, 1, SAMPLE_SEQS, Ts, (cos_s, sin_s) + rt_s + (bd,),
                                 state_ret[l][None], 0, ret_s, l, bo, gn)

        qk_s = lax.slice(P, (Rp, COL_ML), (R, COL_ML + 2 * BR_W)).reshape(Bs, Ts, 2 * BR_W)
        xcs = jnp.concatenate([jnp.zeros((Bs, 1, 2 * BR_W), F32), state_mlstm_conv[l], qk_s], axis=1)
        hm_p, c_p, n_p, m_p, conv_p = _mlstm(P, 0, Bp, Tp // Lp, 1, Lp, jnp.zeros((1, 8, 2 * BR_W), F32), lp,
                                             (tri_p, bd, eye), zeros_heads, 0, c_p, l,
                                             jnp.zeros((Bp, 1, BR_W), F32), jnp.zeros((Bp, 1, 128), F32), bo)
        m0_s = jnp.concatenate([state_mlstm_m[l], jnp.zeros((Bs, 128 - N_HEADS), F32)], axis=1)[:, None]
        hm_s, c_s, n_s, m_s, conv_s = _mlstm(P, Rp, Bs // SAMPLE_SEQS, 1, SAMPLE_SEQS, Ts, xcs, lp,
                                             (tri_s, bd, eye), state_mlstm_C[l][None], 0, c_s, l,
                                             state_mlstm_n[l].reshape(Bs, 1, BR_W), m0_s, bo)

        x = _mix(x, (ha_p, hr_p, hm_p), (ha_s, hr_s, hm_s), P, wbb[l], b_gate[l][:, None], wob[l], tm_mix)
        x = _ffn(x, norm_ffn[l][None], w1b[l], w2b[l], norm_final[None], tm_in, 512, l == DEPTH - 1)

        for i, t in enumerate((shift_p, n_p.reshape(Bp, N_HEADS, HEAD_DIM), m_p[:, 0, :N_HEADS], conv_p,
                               shift_s, n_s.reshape(Bs, N_HEADS, HEAD_DIM), m_s[:, 0, :N_HEADS], conv_s)):
            outs[i].append(t)

    y_p = x[:Rp].reshape(Bp, Tp, D_MODEL)
    y_s = x[Rp:].reshape(Bs, Ts, D_MODEL)
    sh_p, n_p, m_p, cv_p, sh_s, n_s, m_s, cv_s = (jnp.stack(o, axis=0) for o in outs)
    return (y_p, y_s, wkv_p, sh_p, ret_p, c_p, n_p, m_p, cv_p, wkv_s, sh_s, ret_s, c_s, n_s, m_s, cv_s)
```

```python
import functools
import math

import jax
import jax.numpy as jnp
from jax import lax
from jax.experimental import pallas as pl
from jax.experimental.pallas import tpu as pltpu

F32 = jnp.float32
BF16 = jnp.bfloat16

D_MODEL = 1024
DEPTH = 4
HEAD_DIM = 64
N_HEADS = 8
BR_W = N_HEADS * HEAD_DIM
W_LORA = 64
A_LORA = 64
G_LORA = 128
CONV_W = 4
D_FF = 4 * D_MODEL
ROPE_BASE = 10000.0
RMS_EPS = 1e-6
RWKV_GN_EPS = 64e-5
GN_EPS = 1e-5
RWKV_PROJ = 3 * BR_W + W_LORA + A_LORA + G_LORA

COL_RET = 0
COL_ML = 2048
COL_RW = 4096
COL_IF = COL_RW + RWKV_PROJ
COL_GATE = 6144
N_PROJ = COL_GATE + 3 * D_MODEL

PROMPT_CHUNK = 128
RET_CHUNK = 256
SAMPLE_SEQS = 8
SCAN_BATCH = 8
SCAN_TBLK = 64
SCAN_SUB = 8
VMEM_LIMIT = 56 * 1024 * 1024


def _cparams(sem):
    return pltpu.CompilerParams(dimension_semantics=sem, vmem_limit_bytes=VMEM_LIMIT)


def _dot(a, b):
    return jnp.dot(a, b, preferred_element_type=F32)


def _dot_nt(a, b):
    return lax.dot_general(a, b, (((1,), (1,)), ((), ())), preferred_element_type=F32)


def _dot_tn(a, b):
    return lax.dot_general(a, b, (((0,), (0,)), ((), ())), preferred_element_type=F32)


def _split2(x):
    hi = x.astype(BF16)
    lo = (x - hi.astype(F32)).astype(BF16)
    return hi, lo


def _split3(x):
    x1 = x.astype(BF16)
    r1 = x - x1.astype(F32)
    x2 = r1.astype(BF16)
    x3 = (r1 - x2.astype(F32)).astype(BF16)
    return x1, x2, x3


def _sigmoid(x):
    return 1.0 / (1.0 + jnp.exp(-x))


def _softplus(x):
    return jnp.maximum(x, 0.0) + jnp.log(1.0 + jnp.exp(-jnp.abs(x)))


def _seg_bf16(xb, bo):
    return jnp.concatenate([_dot(xb[:, :256], bo), _dot(xb[:, 256:], bo)], axis=1)


def _segsum(x, bo):
    hi, lo = _split2(x)
    return _seg_bf16(hi, bo) + _seg_bf16(lo, bo)


def _head_norm(h, bo, eps):
    mu = _segsum(h, bo) * (1.0 / HEAD_DIM)
    d = h - mu
    var = _segsum(d * d, bo) * (1.0 / HEAD_DIM)
    return d * lax.rsqrt(var + eps)


def _rms(x, g):
    return x * lax.rsqrt(jnp.mean(x * x, axis=-1, keepdims=True) + RMS_EPS) * g


def _inproj_kernel(x_ref, g_ref, w_ref, o_ref, xn_scr):
    @pl.when(pl.program_id(1) == 0)
    def _():
        xn_scr[...] = _rms(x_ref[...], g_ref[...]).astype(BF16)

    o_ref[...] = _dot(xn_scr[...], w_ref[...])


def _inproj(x, g, w, tm, tn):
    R = x.shape[0]
    return pl.pallas_call(
        _inproj_kernel,
        grid=(R // tm, N_PROJ // tn),
        in_specs=[pl.BlockSpec((tm, D_MODEL), lambda i, j: (i, 0)),
                  pl.BlockSpec((1, D_MODEL), lambda i, j: (0, 0)),
                  pl.BlockSpec((D_MODEL, tn), lambda i, j: (0, j))],
        out_specs=pl.BlockSpec((tm, tn), lambda i, j: (i, j)),
        out_shape=jax.ShapeDtypeStruct((R, N_PROJ), F32),
        scratch_shapes=[pltpu.VMEM((tm, D_MODEL), BF16)],
        compiler_params=_cparams(("parallel", "arbitrary")),
        name="inproj",
    )(x, g, w)


def _rwkv_kernel(*refs, nb, tb, n_p, sample):
    p_refs = refs[:n_p]
    (fix_ref, s0_ref, mu_ref, w0_ref, w2_ref, a0_ref, a2_ref, g2_ref, kkp_ref, kap_ref, rk_ref, bo_ref, mk_ref,
     lng_ref, lnb_ref, _, ha_ref, sT_ref, shift_ref, S_scr, seq_scr, o_scr, carry_scr, o8_scr,
     vc_scr) = refs[n_p:]
    sub = min(SCAN_SUB, tb)
    j = pl.program_id(1)
    rows = nb * tb

    @pl.when(j == 0)
    def _():
        for b in range(nb):
            for h in range(N_HEADS):
                S_scr[b, :, h * HEAD_DIM:(h + 1) * HEAD_DIM] = s0_ref[b, h]
        carry_scr[...] = jnp.zeros_like(carry_scr)

    if sample:
        x = p_refs[0][:, :RWKV_PROJ]
        tok = lax.broadcasted_iota(jnp.int32, (rows, 1), 0) & (tb - 1)
        prev = jnp.where(tok == 0, fix_ref[...], pltpu.roll(x, 1, axis=0))
    else:
        first = lax.broadcasted_iota(jnp.int32, (tb, 1), 0) == 0
        xs = [p_refs[b][:, :RWKV_PROJ] for b in range(nb)]
        prev = jnp.concatenate([jnp.where(first, carry_scr[b:b + 1, :], pltpu.roll(xs[b], 1, axis=0))
                                for b in range(nb)], axis=0)
        for b in range(nb):
            carry_scr[b:b + 1, :] = xs[b][tb - 1:tb, :]
        x = jnp.concatenate(xs, axis=0)
    pm = x + (prev - x) * mu_ref[...]
    r = pm[:, 0:512]
    k = pm[:, 512:1024]
    v = pm[:, 1024:1536]
    wa = pm[:, 1536:1664]
    gd = pm[:, 1664:1792]
    bo = bo_ref[...]
    lw = _dot(jnp.tanh(wa).astype(BF16), w2_ref[...])
    w_log = -_softplus(-(w0_ref[...] + lw)) - 0.5
    a = _sigmoid(a0_ref[...] + _dot(wa.astype(BF16), a2_ref[...]))
    kk = k * kkp_ref[...]
    kk = kk / jnp.maximum(jnp.sqrt(_segsum(kk * kk, bo)), 1e-12)
    kt = k * (1.0 + (a - 1.0) * kap_ref[...])
    seq_scr[0] = r
    seq_scr[1] = jnp.exp(-jnp.exp(w_log))
    seq_scr[2] = kt
    seq_scr[3] = -(kk * a)
    seq_scr[4] = kk
    seq_scr[6] = _dot(_sigmoid(gd).astype(BF16), g2_ref[...])
    seq_scr[7] = _segsum(r * kt * rk_ref[...], bo) * v
    v_hi = v.astype(BF16).astype(F32)
    v_lo = _swap_halves(v - v_hi)
    low_half = (lax.broadcasted_iota(jnp.int32, (1, BR_W), 1) & (HEAD_DIM - 1)) < HEAD_DIM // 2
    seq_scr[5] = jnp.where(low_half, v_hi, v_lo)
    seq_scr[8] = jnp.where(low_half, v_lo, v_hi)

    blk = lambda arr, b: arr[b * HEAD_DIM:(b + 1) * HEAD_DIM]
    half = HEAD_DIM // 2
    head_mask = (lax.broadcasted_iota(jnp.int32, (N_HEADS, BR_W), 1) // HEAD_DIM
                 == lax.broadcasted_iota(jnp.int32, (N_HEADS, BR_W), 0)).astype(F32)

    def emit_out(t, s_bf):
        for b in range(nb):
            r8 = (seq_scr[0, pl.ds(b * tb + t, 1), :] * head_mask).astype(BF16)
            o8 = _dot_nt(r8, s_bf[b])
            o8_scr[pl.ds(pl.multiple_of((b * tb + t) * N_HEADS, N_HEADS), N_HEADS), :] = o8

    def step(t, tt):
        row = lambda q, b: seq_scr[q, pl.ds(b * tb + t, 1), :]
        s_bf = [S_scr[b].astype(BF16) for b in range(nb)]
        emit_out(jnp.maximum(t - 1, 0), s_bf)
        sk = _seg_bf16(jnp.concatenate([s_bf[b] * row(4, b).astype(BF16) for b in range(nb)], axis=0), bo)
        for b in range(nb):
            vc = vc_scr[tt, b * HEAD_DIM:(b + 1) * HEAD_DIM, :]
            S_scr[b] = S_scr[b] * row(1, b) + blk(sk, b) * row(3, b) + vc * row(2, b)

    def sub_block(sb, carry):
        t0 = sb * sub
        lhs = []
        for tt in range(sub):
            for b in range(nb):
                lhs.append((seq_scr[5, pl.ds(b * tb + t0 + tt, 1), :] * mk_ref[:half]).astype(BF16))
                lhs.append((seq_scr[8, pl.ds(b * tb + t0 + tt, 1), :] * mk_ref[half:]).astype(BF16))
        vc_all = _seg_bf16(jnp.concatenate(lhs, axis=0), bo)
        for tt in range(sub):
            vc_scr[tt] = vc_all[tt * nb * HEAD_DIM:(tt + 1) * nb * HEAD_DIM]
        for tt in range(sub):
            step(t0 + tt, tt)
        return carry

    lax.fori_loop(0, tb // sub, sub_block, 0)
    emit_out(tb - 1, [S_scr[b].astype(BF16) for b in range(nb)])

    for h in range(N_HEADS):
        o_scr[:, h * HEAD_DIM:(h + 1) * HEAD_DIM] = o8_scr[pl.ds(h, rows, stride=N_HEADS), :]

    o = _head_norm(o_scr[...], bo, RWKV_GN_EPS) * lng_ref[...] + lnb_ref[...]
    ha = (o + seq_scr[7]) * seq_scr[6]
    if sample:
        ha_ref[...] = ha
    else:
        for b in range(nb):
            ha_ref[b] = ha[b * tb:(b + 1) * tb]

    @pl.when(j == pl.num_programs(1) - 1)
    def _():
        for b in range(nb):
            for h in range(N_HEADS):
                sT_ref[b, h] = S_scr[b, :, h * HEAD_DIM:(h + 1) * HEAD_DIM]
            shift_ref[b:b + 1, :] = x[b * tb + tb - 1:b * tb + tb, :]


def _rwkv(P, row0, B, T, tb, fix, s0, l_in, acc, l_out, lp, bo, sample):
    nb = SCAN_BATCH
    cb = COL_RW // 2048
    if sample:
        assert tb == T
        p_specs = [pl.BlockSpec((nb * T, 2048), lambda i, j: (row0 // (nb * T) + i, cb))]
        fix_spec = pl.BlockSpec((nb * T, RWKV_PROJ), lambda i, j: (i, 0))
        ha_spec = pl.BlockSpec((nb * T, BR_W), lambda i, j: (i, 0))
        ha_shape = jax.ShapeDtypeStruct((B * T, BR_W), F32)
    else:
        p_specs = [pl.BlockSpec((tb, 2048), lambda i, j, b=b: (row0 // tb + (i * nb + b) * (T // tb) + j, cb))
                   for b in range(nb)]
        fix_spec = pl.BlockSpec(fix.shape, lambda i, j: (0, 0))
        ha_spec = pl.BlockSpec((nb, tb, BR_W), lambda i, j: (i, j, 0))
        ha_shape = jax.ShapeDtypeStruct((B, T, BR_W), F32)
    row = lambda n: pl.BlockSpec((1, n), lambda i, j: (0, 0))
    full = lambda a, b: pl.BlockSpec((a, b), lambda i, j: (0, 0))
    st_in, st_out = _state_specs(nb, l_in, l_out)
    kern = functools.partial(_rwkv_kernel, nb=nb, tb=tb, n_p=len(p_specs), sample=sample)
    in_specs = p_specs + [fix_spec, st_in, row(RWKV_PROJ), row(BR_W), full(128, BR_W), row(BR_W), full(128, BR_W),
                          full(128, BR_W), row(BR_W), row(BR_W), row(BR_W), full(256, 256),
                          full(HEAD_DIM, BR_W), row(BR_W), row(BR_W), pl.BlockSpec(memory_space=pl.ANY)]
    return pl.pallas_call(
        kern,
        grid=(B // nb, T // tb),
        in_specs=in_specs,
        out_specs=[ha_spec, st_out, pl.BlockSpec((nb, RWKV_PROJ), lambda i, j: (i, 0))],
        out_shape=[ha_shape, jax.ShapeDtypeStruct(acc.shape, F32), jax.ShapeDtypeStruct((B, RWKV_PROJ), F32)],
        input_output_aliases={len(in_specs) - 1: 1},
        scratch_shapes=[pltpu.VMEM((nb, HEAD_DIM, BR_W), F32), pltpu.VMEM((9, nb * tb, BR_W), F32),
                        pltpu.VMEM((nb * tb, BR_W), F32), pltpu.VMEM((nb, RWKV_PROJ), F32),
                        pltpu.VMEM((nb * tb * N_HEADS, HEAD_DIM), F32),
                        pltpu.VMEM((min(SCAN_SUB, tb), nb * HEAD_DIM, BR_W), F32)],
        compiler_params=_cparams(("parallel", "arbitrary")),
        name="rwkv",
    )(*([P] * len(p_specs)), fix, s0, lp['mu'], lp['w0'], lp['w2'], lp['a0'], lp['a2'], lp['g2'], lp['kk'],
      lp['ka'], lp['rk'], bo, _diag_mask(), lp['ln_g'], lp['ln_b'], acc)


def _state_specs(nseq, l_in, l_out):
    mk = lambda l: pl.BlockSpec((None, nseq, N_HEADS, HEAD_DIM, HEAD_DIM), lambda i, c: (l, i, 0, 0, 0))
    return mk(l_in), mk(l_out)


def _diag_mask():
    u = jnp.arange(HEAD_DIM)[:, None]
    j = jnp.arange(BR_W)[None, :] % HEAD_DIM
    return ((j == u) | (j == (u + HEAD_DIM // 2) % HEAD_DIM)).astype(F32)


def _pair_masks():
    lane = lax.broadcasted_iota(jnp.int32, (1, 128), 1)
    m0 = (lane < HEAD_DIM).astype(F32)
    return m0, 1.0 - m0


def _swap_halves(x):
    lane = lax.broadcasted_iota(jnp.int32, x.shape, 1)
    n = x.shape[1]
    return jnp.where((lane & 63) < 32, pltpu.roll(x, n - 32, axis=1), pltpu.roll(x, 32, axis=1))


def _ret_kernel(q_ref, k_ref, v_ref, z_ref, cos_ref, sin_ref, dm_ref, qin_ref, kend_ref, gs_ref, bd_ref, s0_ref,
                bo_ref, gn_ref, _, h_ref, sT_ref, S_scr, *, nseq, L):
    c = pl.program_id(1)

    @pl.when(c == 0)
    def _():
        _load_pairs(S_scr, s0_ref, nseq)

    cos = cos_ref[...]
    sin = sin_ref[...]
    q = q_ref[...]
    k = k_ref[...]
    q = q * cos + _swap_halves(q) * sin
    k = (k * cos + _swap_halves(k) * sin) * (HEAD_DIM ** -0.5)
    v = v_ref[...]
    qin = qin_ref[...]
    ke = k * kend_ref[...]
    m0, m1 = _pair_masks()
    rows = nseq * L
    rowi = lax.broadcasted_iota(jnp.int32, (rows, 1), 0)
    outs = []
    for p in range(N_HEADS // 2):
        sl = slice(128 * p, 128 * (p + 1))
        qp, kp, vp, kep = q[:, sl], k[:, sl], v[:, sl], ke[:, sl]
        kb = kp.astype(BF16)
        o = jnp.zeros((rows, 128), F32)
        for e, me in enumerate((m0, m1)):
            sc = _dot_nt((qp * me).astype(BF16), kb) * dm_ref[2 * p + e]
            o = o + _dot(sc.astype(BF16), (vp * me).astype(BF16))
        qb = qp.astype(BF16)
        for s in range(nseq):
            S = S_scr[s, p]
            oi = _dot(qb, S.astype(BF16)) * qin[:, sl]
            if nseq == 1:
                o = o + oi
                kes = kep
            else:
                inseq = (rowi >= s * L) & (rowi < (s + 1) * L)
                o = o + jnp.where(inseq, oi, 0.0)
                kes = jnp.where(inseq, kep, 0.0)
            S_scr[s, p] = S * gs_ref[p] + bd_ref[...] * _dot_tn(kes.astype(BF16), vp.astype(BF16))
        outs.append(o)
    o = jnp.concatenate(outs, axis=1)
    o = _head_norm(o, bo_ref[...], GN_EPS) * gn_ref[...]
    z = z_ref[...]
    h_ref[...] = z * _sigmoid(z) * o

    @pl.when(c == pl.num_programs(1) - 1)
    def _():
        _store_pairs(sT_ref, S_scr, nseq)


def _load_pairs(S_scr, s0_ref, nseq):
    S_scr[...] = jnp.zeros_like(S_scr)
    for s in range(nseq):
        for p in range(N_HEADS // 2):
            S_scr[s, p, 0:HEAD_DIM, 0:HEAD_DIM] = s0_ref[s, 2 * p]
            S_scr[s, p, HEAD_DIM:, HEAD_DIM:] = s0_ref[s, 2 * p + 1]


def _store_pairs(sT_ref, S_scr, nseq):
    for s in range(nseq):
        for p in range(N_HEADS // 2):
            sT_ref[s, 2 * p] = S_scr[s, p, 0:HEAD_DIM, 0:HEAD_DIM]
            sT_ref[s, 2 * p + 1] = S_scr[s, p, HEAD_DIM:, HEAD_DIM:]


def _retention(P, row0, n_groups, n_chunks, nseq, L, tabs, s0, l_in, acc, l_out, bo, gn):
    rows = nseq * L
    rb0 = row0 // rows
    cos, sin, dm, qin, kend, gs, bd = tabs
    col = lambda cb: pl.BlockSpec((rows, BR_W), lambda i, c: (rb0 + i * n_chunks + c, cb))
    tab = pl.BlockSpec((rows, BR_W), lambda i, c: (c, 0))
    cst = lambda shp: pl.BlockSpec(shp, lambda i, c: (0,) * len(shp))
    st_in, st_out = _state_specs(nseq, l_in, l_out)
    kern = functools.partial(_ret_kernel, nseq=nseq, L=L)
    in_specs = [col(0), col(1), col(2), col(3), tab, tab, cst((N_HEADS, rows, rows)), cst((rows, BR_W)),
                cst((rows, BR_W)), cst((4, 128, 128)), cst((128, 128)), st_in, cst((256, 256)), cst((1, BR_W)),
                pl.BlockSpec(memory_space=pl.ANY)]
    return pl.pallas_call(
        kern,
        grid=(n_groups, n_chunks),
        in_specs=in_specs,
        out_specs=[pl.BlockSpec((rows, BR_W), lambda i, c: (i * n_chunks + c, 0)), st_out],
        out_shape=[jax.ShapeDtypeStruct((n_groups * n_chunks * rows, BR_W), F32),
                   jax.ShapeDtypeStruct(acc.shape, F32)],
        input_output_aliases={len(in_specs) - 1: 1},
        scratch_shapes=[pltpu.VMEM((nseq, 4, 128, 128), F32)],
        compiler_params=_cparams(("parallel", "arbitrary")),
        name="retention",
    )(P, P, P, P, cos, sin, dm, qin, kend, gs, bd, s0, bo, gn, acc)


def _mlstm_kernel(qk_ref, v_ref, o_ref, if_ref, xcs_ref, cw_ref, cb_ref, bi_ref, bf_ref, tri_ref, bd_ref, eye_ref,
                  c0_ref, n0_ref, m0_ref, bo_ref, gn_ref, _, h_ref, cT_ref, nT_ref, mT_ref, convT_ref,
                  C_scr, n_scr, m_scr, xc_scr, *, nseq, L):
    c = pl.program_id(1)
    rows = nseq * L

    @pl.when(c == 0)
    def _():
        _load_pairs(C_scr, c0_ref, nseq)
        n_scr[...] = n0_ref[...]
        m_scr[...] = m0_ref[...]
        if nseq == 1:
            xc_scr[0:8, :] = jnp.zeros((8, 2 * BR_W), F32)

    cw = cw_ref[...]
    if nseq == 1:
        xc_scr[8:8 + L, :] = qk_ref[...]
        conv = cb_ref[...]
        for w in range(CONV_W):
            conv = conv + xc_scr[5 + w:5 + w + L, :] * cw[w:w + 1, :]
        xc_scr[0:8, :] = xc_scr[L:L + 8, :]
    else:
        parts = []
        for s in range(nseq):
            cs = cb_ref[...]
            for w in range(CONV_W):
                cs = cs + xcs_ref[s, 1 + w:1 + w + L, :] * cw[w:w + 1, :]
            parts.append(cs)
        conv = jnp.concatenate(parts, axis=0)
    qk = conv * _sigmoid(conv)
    q = qk[:, :BR_W]
    k = qk[:, BR_W:] * (HEAD_DIM ** -0.5)
    v = v_ref[...]

    rowi = lax.broadcasted_iota(jnp.int32, (rows, 1), 0)
    tok = rowi % L
    ig = if_ref[:, 0:128] + bi_ref[...]
    fp = if_ref[:, 128:256] + bf_ref[...]
    lf = jnp.minimum(fp, 0.0) - jnp.log(1.0 + jnp.exp(-jnp.abs(fp)))
    tri = tri_ref[...]
    l1, l2, l3 = _split3(lf)
    bcum = _dot(tri, l1) + _dot(tri, l2) + _dot(tri, l3)
    u = ig - bcum
    cm = u
    sh = 1
    while sh < L:
        cm = jnp.maximum(cm, jnp.where(tok >= sh, pltpu.roll(cm, sh, axis=0), -jnp.inf))
        sh *= 2
    if nseq == 1:
        m0r = m_scr[0]
    else:
        m0r = jnp.concatenate([jnp.broadcast_to(m_scr[s], (L, 128)) for s in range(nseq)], axis=0)
    mt = bcum + jnp.maximum(m0r, cm)
    inter = jnp.exp(bcum + m0r - mt)
    bmm = bcum - mt
    emt = jnp.exp(-mt)
    wend_parts = []
    for s in range(nseq):
        last = slice(s * L + L - 1, s * L + L)
        m_end = mt[last, :]
        wend_parts.append(jnp.exp(bcum[last, :] - m_end + u[s * L:(s + 1) * L, :]))
    wend = wend_parts[0] if nseq == 1 else jnp.concatenate(wend_parts, axis=0)
    eye = eye_ref[...]
    u1, u2, u3 = _split3(u)
    ut = _dot_nt(eye, u1) + _dot_nt(eye, u2) + _dot_nt(eye, u3)
    trib = tri > 0.5

    m0m, m1m = _pair_masks()
    lane128 = lax.broadcasted_iota(jnp.int32, (1, 128), 1)
    first = lane128 < HEAD_DIM
    bo = bo_ref[...]
    qn_all = None
    outs = []
    wk_all = []
    for p in range(N_HEADS // 2):
        sl = slice(128 * p, 128 * (p + 1))
        qp, kp, vp = q[:, sl], k[:, sl], v[:, sl]
        kb = kp.astype(BF16)
        num = jnp.zeros((rows, 128), F32)
        dsum = []
        for e, me in enumerate((m0m, m1m)):
            h = 2 * p + e
            logd = bmm[:, h:h + 1] + ut[h:h + 1, :]
            dmat = jnp.where(trib, jnp.exp(jnp.minimum(logd, 0.0)), 0.0)
            sc = _dot_nt((qp * me).astype(BF16), kb) * dmat
            num = num + _dot(sc.astype(BF16), (vp * me).astype(BF16))
            dsum.append(jnp.sum(sc, axis=1, keepdims=True))
        pick = lambda arr: jnp.where(first, arr[:, 2 * p:2 * p + 1], arr[:, 2 * p + 1:2 * p + 2])
        inter_p = pick(inter)
        den = jnp.where(first, dsum[0], dsum[1])
        wend_p = pick(wend)
        qb = qp.astype(BF16)
        kw = kp * wend_p
        wk_all.append(kw)
        qn = jnp.zeros((rows, 128), F32)
        for s in range(nseq):
            C = C_scr[s, p]
            nrow = n_scr[s][:, sl]
            qc = _dot(qb, C.astype(BF16))
            qns = qp * nrow
            if nseq == 1:
                num = num + inter_p * qc
                qn = qns
                kws = kw
                send = inter_p[L - 1:L, :]
            else:
                inseq = (rowi >= s * L) & (rowi < (s + 1) * L)
                num = num + jnp.where(inseq, inter_p * qc, 0.0)
                qn = qn + jnp.where(inseq, qns, 0.0)
                kws = jnp.where(inseq, kw, 0.0)
                send = inter_p[s * L + L - 1:s * L + L, :]
            send_col = jnp.where(lax.broadcasted_iota(jnp.int32, (128, 1), 0) < HEAD_DIM,
                                 send[:, 0:1], send[:, 64:65])
            C_scr[s, p] = C * send_col + bd_ref[...] * _dot_tn(kws.astype(BF16), vp.astype(BF16))
        den = den + inter_p * _segsum_pair(qn, bo)
        outs.append(num / jnp.maximum(jnp.abs(den), pick(emt)))
    hh = jnp.concatenate(outs, axis=1)
    kw_full = jnp.concatenate(wk_all, axis=1)
    for s in range(nseq):
        last = slice(s * L + L - 1, s * L + L)
        send_row = _expand_heads(inter[last, :])
        n_scr[s] = send_row * n_scr[s] + jnp.sum(kw_full[s * L:(s + 1) * L, :], axis=0, keepdims=True)
        m_scr[s] = mt[last, :]
    op = o_ref[...]
    hh = _sigmoid(op) * hh
    h_ref[...] = _head_norm(hh, bo, GN_EPS) * gn_ref[...]

    @pl.when(c == pl.num_programs(1) - 1)
    def _():
        _store_pairs(cT_ref, C_scr, nseq)
        nT_ref[...] = n_scr[...]
        mT_ref[...] = m_scr[...]
        if nseq == 1:
            convT_ref[0] = xc_scr[8 - (CONV_W - 1):8, :]
        else:
            for s in range(nseq):
                convT_ref[s] = xcs_ref[s, 8 - (CONV_W - 1):8, :]


def _segsum_pair(x, bo):
    hi, lo = _split2(x)
    b2 = bo[:128, :128]
    return _dot(hi, b2) + _dot(lo, b2)


def _expand_heads(row):
    lane = lax.broadcasted_iota(jnp.int32, (1, BR_W), 1)
    out = jnp.zeros((1, BR_W), F32)
    for h in range(N_HEADS):
        out = jnp.where((lane >> 6) == h, row[:, h:h + 1], out)
    return out


def _mlstm(P, row0, n_groups, n_chunks, nseq, L, xcs, lp, tabs, c0, l_in, acc, l_out, n0, m0, bo):
    rows = nseq * L
    rb0 = row0 // rows
    tri, bd, eye = tabs
    rowblk = lambda width, cb: pl.BlockSpec((rows, width), lambda i, c: (rb0 + i * n_chunks + c, cb))
    cst = lambda shp: pl.BlockSpec(shp, lambda i, c: (0,) * len(shp))
    stC_in, stC_out = _state_specs(nseq, l_in, l_out)
    stn = pl.BlockSpec((nseq, 1, BR_W), lambda i, c: (i, 0, 0))
    stm = pl.BlockSpec((nseq, 1, 128), lambda i, c: (i, 0, 0))
    if nseq == 1:
        xcs_spec = cst((1, 8, 2 * BR_W))
    else:
        xcs_spec = pl.BlockSpec((nseq, 8, 2 * BR_W), lambda i, c: (i, 0, 0))
    kern = functools.partial(_mlstm_kernel, nseq=nseq, L=L)
    nB = n_groups * nseq
    in_specs = [rowblk(2 * BR_W, COL_ML // (2 * BR_W)), rowblk(BR_W, (COL_ML + 1024) // BR_W),
                rowblk(BR_W, (COL_ML + 1536) // BR_W), rowblk(256, COL_IF // 256), xcs_spec,
                cst((CONV_W, 2 * BR_W)), cst((1, 2 * BR_W)), cst((1, 128)), cst((1, 128)),
                cst((rows, rows)), cst((128, 128)), cst((128, 128)), stC_in, stn, stm, cst((256, 256)),
                cst((1, BR_W)), pl.BlockSpec(memory_space=pl.ANY)]
    return pl.pallas_call(
        kern,
        grid=(n_groups, n_chunks),
        in_specs=in_specs,
        out_specs=[pl.BlockSpec((rows, BR_W), lambda i, c: (i * n_chunks + c, 0)), stC_out, stn, stm,
                   pl.BlockSpec((nseq, CONV_W - 1, 2 * BR_W), lambda i, c: (i, 0, 0))],
        input_output_aliases={len(in_specs) - 1: 1},
        out_shape=[jax.ShapeDtypeStruct((n_groups * n_chunks * rows, BR_W), F32),
                   jax.ShapeDtypeStruct(acc.shape, F32),
                   jax.ShapeDtypeStruct((nB, 1, BR_W), F32),
                   jax.ShapeDtypeStruct((nB, 1, 128), F32),
                   jax.ShapeDtypeStruct((nB, CONV_W - 1, 2 * BR_W), F32)],
        scratch_shapes=[pltpu.VMEM((nseq, 4, 128, 128), F32), pltpu.VMEM((nseq, 1, BR_W), F32),
                        pltpu.VMEM((nseq, 1, 128), F32), pltpu.VMEM((L + 8, 2 * BR_W), F32)],
        compiler_params=_cparams(("parallel", "arbitrary")),
        name="mlstm",
    )(P, P, P, P, xcs, lp['conv_w'], lp['conv_b'], lp['b_i'], lp['b_f'], tri, bd, eye, c0, n0, m0, bo,
      lp['mlstm_gn'], acc)


def _mix_kernel(x_ref, hap_ref, has_ref, hrp_ref, hrs_ref, hmp_ref, hms_ref, g0_ref, g1_ref, g2_ref, wb_ref,
                bg_ref, wo_ref, o_ref, *, n_p):
    def body(h_refs):
        mixed = None
        for n, (h_ref, pg_ref) in enumerate(zip(h_refs, (g0_ref, g1_ref, g2_ref))):
            ub = _dot(h_ref[...].astype(BF16), wb_ref[n])
            t = _sigmoid(pg_ref[...] + bg_ref[n]) * ub
            mixed = t if mixed is None else mixed + t
        o_ref[...] = x_ref[...] + _dot(mixed.astype(BF16), wo_ref[...])

    @pl.when(pl.program_id(0) < n_p)
    def _():
        body((hap_ref, hrp_ref, hmp_ref))

    @pl.when(pl.program_id(0) >= n_p)
    def _():
        body((has_ref, hrs_ref, hms_ref))


def _mix(x, hp, hs, P, wb, bg, wo, tm):
    R = x.shape[0]
    n_p = hp[0].shape[0] // tm
    xs = pl.BlockSpec((tm, D_MODEL), lambda i: (i, 0))
    hps = pl.BlockSpec((tm, BR_W), lambda i: (jnp.minimum(i, n_p - 1), 0))
    hss = pl.BlockSpec((tm, BR_W), lambda i: (jnp.maximum(i - n_p, 0), 0))
    gs = lambda n: pl.BlockSpec((tm, D_MODEL), lambda i: (i, COL_GATE // D_MODEL + n))
    return pl.pallas_call(
        functools.partial(_mix_kernel, n_p=n_p),
        grid=(R // tm,),
        in_specs=[xs, hps, hss, hps, hss, hps, hss, gs(0), gs(1), gs(2),
                  pl.BlockSpec((3, BR_W, D_MODEL), lambda i: (0, 0, 0)),
                  pl.BlockSpec((3, 1, D_MODEL), lambda i: (0, 0, 0)),
                  pl.BlockSpec((D_MODEL, D_MODEL), lambda i: (0, 0))],
        out_specs=xs,
        out_shape=jax.ShapeDtypeStruct((R, D_MODEL), F32),
        compiler_params=_cparams(("parallel",)),
        name="mix_out",
    )(x, hp[0], hs[0], hp[1], hs[1], hp[2], hs[2], P, P, P, wb, bg, wo)


def _ffn_kernel(x_ref, g_ref, w1_ref, w2_ref, gf_ref, o_ref, xn_scr, acc_scr, *, final):
    j = pl.program_id(1)

    @pl.when(j == 0)
    def _():
        xn_scr[...] = _rms(x_ref[...], g_ref[...]).astype(BF16)
        acc_scr[...] = jnp.zeros_like(acc_scr)

    h = jnp.maximum(_dot(xn_scr[...], w1_ref[...]), 0.0)
    acc_scr[...] += _dot((h * h).astype(BF16), w2_ref[...])

    @pl.when(j == pl.num_programs(1) - 1)
    def _():
        y = x_ref[...] + acc_scr[...]
        if final:
            y = _rms(y, gf_ref[...])
        o_ref[...] = y


def _ffn(x, g, w1, w2, gf, tm, tf, final):
    R = x.shape[0]
    kern = functools.partial(_ffn_kernel, final=final)
    return pl.pallas_call(
        kern,
        grid=(R // tm, D_FF // tf),
        in_specs=[pl.BlockSpec((tm, D_MODEL), lambda i, j: (i, 0)),
                  pl.BlockSpec((1, D_MODEL), lambda i, j: (0, 0)),
                  pl.BlockSpec((D_MODEL, tf), lambda i, j: (0, j)),
                  pl.BlockSpec((tf, D_MODEL), lambda i, j: (j, 0)),
                  pl.BlockSpec((1, D_MODEL), lambda i, j: (0, 0))],
        out_specs=pl.BlockSpec((tm, D_MODEL), lambda i, j: (i, 0)),
        out_shape=jax.ShapeDtypeStruct((R, D_MODEL), F32),
        scratch_shapes=[pltpu.VMEM((tm, D_MODEL), BF16), pltpu.VMEM((tm, D_MODEL), F32)],
        compiler_params=_cparams(("parallel", "arbitrary")),
        name="ffn",
    )(x, g, w1, w2, gf)


def _rope_tables(T, pos0):
    half = HEAD_DIM // 2
    inv = jnp.power(ROPE_BASE, -jnp.arange(half, dtype=F32) / half)
    ang = (jnp.arange(T, dtype=F32) + pos0)[:, None] * inv[None, :]
    cos, sin = jnp.cos(ang), jnp.sin(ang)
    cos_h = jnp.concatenate([cos, cos], axis=1)
    sin_h = jnp.concatenate([-sin, sin], axis=1)
    return jnp.tile(cos_h, (1, N_HEADS)), jnp.tile(sin_h, (1, N_HEADS))


def _ret_tables(nseq, L):
    log_g = jnp.log(1.0 - jnp.exp2(-5.0 - jnp.arange(N_HEADS, dtype=F32)))
    idx = jnp.arange(L, dtype=F32)
    diff = idx[:, None] - idx[None, :]
    dmask = jnp.where(diff[None] >= 0, jnp.exp(jnp.maximum(diff, 0.0)[None] * log_g[:, None, None]), 0.0)
    q_in = jnp.exp((idx + 1.0)[:, None] * log_g[None, :])
    k_end = jnp.exp((L - 1.0 - idx)[:, None] * log_g[None, :])
    g_chunk = jnp.exp(L * log_g)
    eye = jnp.eye(nseq, dtype=F32)
    dm = jnp.einsum('st,hij->hsitj', eye, dmask).reshape(N_HEADS, nseq * L, nseq * L)
    qin = jnp.tile(jnp.repeat(q_in, HEAD_DIM, axis=1), (nseq, 1))
    kend = jnp.tile(jnp.repeat(k_end, HEAD_DIM, axis=1), (nseq, 1))
    gs = jnp.broadcast_to(jnp.repeat(g_chunk, HEAD_DIM).reshape(4, 128, 1), (4, 128, 128))
    return dm, qin, kend, gs


def _block_diag_ones(n):
    i = jnp.arange(n) // HEAD_DIM
    return (i[:, None] == i[None, :])


def _seq_tri(nseq, L):
    r = jnp.arange(nseq * L)
    return ((r[:, None] // L == r[None, :] // L) & (r[:, None] >= r[None, :])).astype(BF16)


def _pick_tile(n, cands):
    for c in cands:
        if n % c == 0:
            return c
    raise ValueError(f"no tile for {n}")


def kernel(x_prompt, x_sample, state_rwkv_wkv, state_rwkv_shift, state_ret, state_mlstm_C, state_mlstm_n,
           state_mlstm_m, state_mlstm_conv, norm_mix, w_in, rwkv_mu, rwkv_w0, rwkv_w2, rwkv_a0, rwkv_a2,
           rwkv_g2, rwkv_kk, rwkv_ka, rwkv_rk, rwkv_ln_g, rwkv_ln_b, ret_gn_g, mlstm_conv_w, mlstm_conv_b,
           mlstm_b_i, mlstm_b_f, mlstm_gn_g, w_branch, b_gate, w_out, norm_ffn, w_ff1, w_ff2, norm_final):
    Bp, Tp, _ = x_prompt.shape
    Bs, Ts, _ = x_sample.shape
    Rp, Rs = Bp * Tp, Bs * Ts
    R = Rp + Rs
    past_len = 16384
    Lp = min(PROMPT_CHUNK, Tp)
    Lr = min(RET_CHUNK, Tp)
    assert Tp % Lp == 0 and Tp % Lr == 0 and Bs % SAMPLE_SEQS == 0 and Rp % (SAMPLE_SEQS * Ts) == 0
    assert Tp & (Tp - 1) == 0 and Ts & (Ts - 1) == 0 and Tp >= CONV_W - 1 and Bp % SCAN_BATCH == 0

    x = jnp.concatenate([x_prompt.reshape(Rp, D_MODEL), x_sample.reshape(Rs, D_MODEL)], axis=0)

    o_rw, o_ret, o_ml = 0, RWKV_PROJ, RWKV_PROJ + 4 * BR_W
    o_if = o_ml + 4 * BR_W
    o_gate = o_if + 2 * N_HEADS
    zpad = lambda n: jnp.zeros((DEPTH, D_MODEL, n), BF16)
    w_inb = w_in.astype(BF16)
    w_cat = jnp.concatenate([
        w_inb[:, :, o_ret:o_ret + 4 * BR_W], w_inb[:, :, o_ml:o_ml + 4 * BR_W], w_inb[:, :, o_rw:o_rw + RWKV_PROJ],
        w_inb[:, :, o_if:o_if + N_HEADS], zpad(128 - N_HEADS), w_inb[:, :, o_if + N_HEADS:o_gate],
        zpad(128 - N_HEADS), w_inb[:, :, o_gate:]], axis=2)
    pad_rows = lambda a, top: jnp.concatenate(
        [jnp.zeros((DEPTH, top, BR_W), F32), a, jnp.zeros((DEPTH, 128 - top - a.shape[1], BR_W), F32)], axis=1)
    w2p = pad_rows(rwkv_w2, 0).astype(BF16)
    a2p = pad_rows(rwkv_a2, W_LORA).astype(BF16)
    g2b = rwkv_g2.astype(BF16)
    pad128 = lambda a: jnp.concatenate([a, jnp.zeros((DEPTH, 128 - N_HEADS), F32)], axis=1)
    wbb, wob, w1b, w2b = (t.astype(BF16) for t in (w_branch, w_out, w_ff1, w_ff2))
    bo = _block_diag_ones(256).astype(BF16)
    bd = _block_diag_ones(128).astype(F32)
    eye = jnp.eye(128, dtype=BF16)

    cos_p, sin_p = _rope_tables(Tp, 0)
    cos_s, sin_s = _rope_tables(Ts, past_len)
    cos_s, sin_s = jnp.tile(cos_s, (SAMPLE_SEQS, 1)), jnp.tile(sin_s, (SAMPLE_SEQS, 1))
    rt_p = _ret_tables(1, Lr)
    rt_s = _ret_tables(SAMPLE_SEQS, Ts)
    tri_p, tri_s = _seq_tri(1, Lp), _seq_tri(SAMPLE_SEQS, Ts)

    tm_in = _pick_tile(R, (1536, 768, 576, 192, 64))
    tm_mix = _pick_tile(math.gcd(Rp, Rs), (512, 64))
    tb_p = min(SCAN_TBLK, Tp)

    zeros_heads = jnp.zeros((1, Bp, N_HEADS, HEAD_DIM, HEAD_DIM), F32)
    stacked = lambda B: jnp.zeros((DEPTH, B, N_HEADS, HEAD_DIM, HEAD_DIM), F32)
    wkv_p, ret_p, c_p, wkv_s, ret_s, c_s = (stacked(Bp), stacked(Bp), stacked(Bp),
                                            stacked(Bs), stacked(Bs), stacked(Bs))
    outs = [[] for _ in range(8)]
    for l in range(DEPTH):
        lp = dict(mu=rwkv_mu[l][None], w0=rwkv_w0[l][None], w2=w2p[l], a0=rwkv_a0[l][None], a2=a2p[l], g2=g2b[l],
                  kk=rwkv_kk[l][None], ka=rwkv_ka[l][None], rk=rwkv_rk[l].reshape(1, BR_W),
                  ln_g=rwkv_ln_g[l][None], ln_b=rwkv_ln_b[l][None],
                  conv_w=mlstm_conv_w[l], conv_b=mlstm_conv_b[l][None], b_i=pad128(mlstm_b_i)[l][None],
                  b_f=pad128(mlstm_b_f)[l][None], mlstm_gn=mlstm_gn_g[l][None])
        P = _inproj(x, norm_mix[l][None], w_cat[l], tm_in, 1536)

        fix_s = jnp.repeat(state_rwkv_shift[l], Ts, axis=0)
        ha_p, wkv_p, shift_p = _rwkv(P, 0, Bp, Tp, tb_p, jnp.zeros((8, 128), F32), zeros_heads, 0, wkv_p, l,
                                     lp, bo, False)
        ha_s, wkv_s, shift_s = _rwkv(P, Rp, Bs, Ts, Ts, fix_s, state_rwkv_wkv, l, wkv_s, l, lp, bo, True)
        ha_p = ha_p.reshape(Rp, BR_W)

        gn = ret_gn_g[l][None]
        hr_p, ret_p = _retention(P, 0, Bp, Tp // Lr, 1, Lr, (cos_p, sin_p) + rt_p + (bd,), zeros_heads, 0,
                                 ret_p, l, bo, gn)
        hr_s, ret_s = _retention(P, Rp, Bs // SAMPLE_SEQS, 1, SAMPLE_SEQS, Ts, (cos_s, sin_s) + rt_s + (bd,),
                                 state_ret, l, ret_s, l, bo, gn)

        qk_s = lax.slice(P, (Rp, COL_ML), (R, COL_ML + 2 * BR_W)).reshape(Bs, Ts, 2 * BR_W)
        xcs = jnp.concatenate([jnp.zeros((Bs, 1, 2 * BR_W), F32), state_mlstm_conv[l], qk_s], axis=1)
        hm_p, c_p, n_p, m_p, conv_p = _mlstm(P, 0, Bp, Tp // Lp, 1, Lp, jnp.zeros((1, 8, 2 * BR_W), F32), lp,
                                             (tri_p, bd, eye), zeros_heads, 0, c_p, l,
                                             jnp.zeros((Bp, 1, BR_W), F32), jnp.zeros((Bp, 1, 128), F32), bo)
        m0_s = jnp.concatenate([state_mlstm_m[l], jnp.zeros((Bs, 128 - N_HEADS), F32)], axis=1)[:, None]
        hm_s, c_s, n_s, m_s, conv_s = _mlstm(P, Rp, Bs // SAMPLE_SEQS, 1, SAMPLE_SEQS, Ts, xcs, lp,
                                             (tri_s, bd, eye), state_mlstm_C, l, c_s, l,
                                             state_mlstm_n[l].reshape(Bs, 1, BR_W), m0_s, bo)

        x = _mix(x, (ha_p, hr_p, hm_p), (ha_s, hr_s, hm_s), P, wbb[l], b_gate[l][:, None], wob[l], tm_mix)
        x = _ffn(x, norm_ffn[l][None], w1b[l], w2b[l], norm_final[None], tm_in, 512, l == DEPTH - 1)

        for i, t in enumerate((shift_p, n_p.reshape(Bp, N_HEADS, HEAD_DIM), m_p[:, 0, :N_HEADS], conv_p,
                               shift_s, n_s.reshape(Bs, N_HEADS, HEAD_DIM), m_s[:, 0, :N_HEADS], conv_s)):
            outs[i].append(t)

    y_p = x[:Rp].reshape(Bp, Tp, D_MODEL)
    y_s = x[Rp:].reshape(Bs, Ts, D_MODEL)
    sh_p, n_p, m_p, cv_p, sh_s, n_s, m_s, cv_s = (jnp.stack(o, axis=0) for o in outs)
    return (y_p, y_s, wkv_p, sh_p, ret_p, c_p, n_p, m_p, cv_p, wkv_s, sh_s, ret_s, c_s, n_s, m_s, cv_s)
```
